```python
import jax, jax.numpy as jnp
from jax import lax
import numpy as np

D_MODEL = 2048
BATCH = 32
SEQ = 256
DEPTH = 4
DEC_BATCH = 8
DEC_SEQ = 2048
PAST_LEN = 256

GRID_W = 64
MIX_WIDTH = D_MODEL
GLA_WIDTH = MIX_WIDTH // 2
POOL_WIDTH = MIX_WIDTH - GLA_WIDTH
GLA_HEADS = 4
GLA_DV = GLA_WIDTH // GLA_HEADS
GLA_DK = GLA_DV // 2
GLA_KEY_WIDTH = GLA_HEADS * GLA_DK
GLA_GATE_RANK = 16
GLA_GATE_TEMP = 16.0
GLA_CHUNK = 64
POOL_WINDOWS = (2, 4, 8, 16)
POOL_GROUPS = len(POOL_WINDOWS)
POOL_GROUP_DIM = POOL_WIDTH // POOL_GROUPS
D_FF = ((8 * D_MODEL + 2) // 3 + 255) // 256 * 256
ALPHA = (2 * DEPTH) ** 0.25
BETA = (8 * DEPTH) ** -0.25
LN_EPS = 1e-5
Q_OFF = 0
K_OFF = Q_OFF + GLA_KEY_WIDTH
V_OFF = K_OFF + GLA_KEY_WIDTH
G_OFF = V_OFF + GLA_WIDTH
AF_OFF = G_OFF + GLA_WIDTH
AB_OFF = AF_OFF + GLA_GATE_RANK
P_OFF = AB_OFF + GLA_GATE_RANK
IN_WIDTH = P_OFF + POOL_WIDTH

kernel_name = 'hymba_gla_pool_dit_step'


def layer_norm(x, g, b):
    xf = x.astype(jnp.float32)
    mu = jnp.mean(xf, axis=-1, keepdims=True)
    var = jnp.mean(jnp.square(xf - mu), axis=-1, keepdims=True)
    y = (xf - mu) * lax.rsqrt(var + LN_EPS) * g.astype(jnp.float32) + b.astype(jnp.float32)
    return y.astype(x.dtype)


def gla_chunked(q, k, v, log_a, s0):
    B, H, L, dk = q.shape
    dv = v.shape[-1]
    n = L // GLA_CHUNK
    r = lambda t: t.reshape(B, H, n, GLA_CHUNK, t.shape[-1])
    q, k, v, log_a = r(q), r(k), r(v), r(log_a)
    b = jnp.cumsum(log_a, axis=3)
    b_last = b[..., -1:, :]
    qe = q * jnp.exp(b)
    ke = k * jnp.exp(-b)
    kd = k * jnp.exp(b_last - b)
    order = jnp.tril(jnp.ones((GLA_CHUNK, GLA_CHUNK), dtype=bool))
    A = jnp.where(order, jnp.einsum('bhncd,bhnsd->bhncs', qe, ke), 0.0)
    o_intra = jnp.einsum('bhncs,bhnsv->bhncv', A, v)
    U = jnp.einsum('bhncd,bhncv->bhndv', kd, v)
    decay = jnp.exp(b_last[..., 0, :])

    def step(S, inp):
        d, u = inp
        return d[..., None] * S + u, S

    s_final, s_before = lax.scan(step, s0, (jnp.moveaxis(decay, 2, 0), jnp.moveaxis(U, 2, 0)))
    s_before = jnp.moveaxis(s_before, 0, 2)
    o = o_intra + jnp.einsum('bhncd,bhndv->bhncv', qe, s_before)
    return o.reshape(B, H, L, dv), s_final


def _window_bounds(n, w):
    idx = jnp.arange(n)
    lo = jnp.clip(idx - w // 2, 0, n)
    hi = jnp.clip(idx - w // 2 + w, 0, n)
    return lo, hi


def pool_1d(p, w):
    L = p.shape[1]
    P = jnp.concatenate([jnp.zeros_like(p[:, :1]), jnp.cumsum(p, axis=1)], axis=1)
    lo, hi = _window_bounds(L, w)
    cnt = (hi - lo).astype(p.dtype)
    return (P[:, hi] - P[:, lo]) / cnt[None, :, None] - p


def pool_2d(p, w):
    B, L, C = p.shape
    rows = L // GRID_W
    g = p.reshape(B, rows, GRID_W, C)
    P = jnp.pad(jnp.cumsum(jnp.cumsum(g, axis=1), axis=2), ((0, 0), (1, 0), (1, 0), (0, 0)))
    lr, hr = _window_bounds(rows, w)
    lc, hc = _window_bounds(GRID_W, w)
    Ph = P[:, hr]
    Pl = P[:, lr]
    s = Ph[:, :, hc] - Pl[:, :, hc] - Ph[:, :, lc] + Pl[:, :, lc]
    cnt = ((hr - lr)[:, None] * (hc - lc)[None, :]).astype(p.dtype)
    return (s / cnt[None, :, :, None] - g).reshape(B, L, C)


def mixer(h, s0_f, s0_b, on_grid, w_in, w_a2, b_a, gla_norm_g, w_pool, pool_scale, w_out):
    B, L, _ = h.shape
    f32 = jnp.float32
    z = h @ w_in
    q, k, v, g, a_f, a_b, p = jnp.split(z, [K_OFF, V_OFF, G_OFF, AF_OFF, AB_OFF, P_OFF], axis=-1)
    heads = lambda t, d: t.astype(f32).reshape(B, L, GLA_HEADS, d).transpose(0, 2, 1, 3)
    q = heads(q, GLA_DK) * (GLA_DK ** -0.5)
    k = heads(k, GLA_DK)
    v = heads(v, GLA_DV)
    log_a_f = heads(jax.nn.log_sigmoid((a_f @ w_a2[0] + b_a[0]).astype(f32)) / GLA_GATE_TEMP, GLA_DK)
    log_a_b = heads(jax.nn.log_sigmoid((a_b @ w_a2[1] + b_a[1]).astype(f32)) / GLA_GATE_TEMP, GLA_DK)
    flip = lambda t: jnp.flip(t, axis=2)
    o_f, s_f = gla_chunked(q, k, v, log_a_f, s0_f.astype(f32))
    o_b, s_b = gla_chunked(flip(q), flip(k), flip(v), flip(log_a_b), s0_b.astype(f32))
    o = (o_f + flip(o_b)).transpose(0, 2, 1, 3)
    o = o * lax.rsqrt(jnp.mean(jnp.square(o), axis=-1, keepdims=True) + LN_EPS)
    o = o * gla_norm_g.astype(f32).reshape(GLA_HEADS, GLA_DV)
    o = o.reshape(B, L, GLA_WIDTH) * jax.nn.silu(g.astype(f32))
    pf = p.astype(f32)
    pool_fn = pool_2d if on_grid else pool_1d
    pooled = jnp.stack([pool_fn(pf[..., i * POOL_GROUP_DIM:(i + 1) * POOL_GROUP_DIM], w)
                        for i, w in enumerate(POOL_WINDOWS)], axis=2)
    y_pool = jnp.einsum('blgc,gcd->blgd', pooled, w_pool).reshape(B, L, POOL_WIDTH)
    y_pool = y_pool * pool_scale.astype(f32)
    mixed = jnp.concatenate([o, y_pool], axis=-1).astype(h.dtype)
    return mixed @ w_out, s_f, s_b


def block(x, mod, s0_f, s0_b, on_grid, w_in, w_a2, b_a, gla_norm_g, w_pool, pool_scale, w_out,
          ln1_g, ln1_b, w_gu, w_down, ln2_g, ln2_b):
    sh1, sc1, g1, sh2, sc2, g2 = jnp.split(mod, 6, axis=-1)
    h = x * (1 + sc1) + sh1
    m, s_f, s_b = mixer(h, s0_f, s0_b, on_grid, w_in, w_a2, b_a, gla_norm_g, w_pool, pool_scale, w_out)
    x = layer_norm(ALPHA * x + g1 * m, ln1_g, ln1_b)
    h = x * (1 + sc2) + sh2
    gate, up = jnp.split(h @ w_gu, 2, axis=-1)
    f = (jax.nn.silu(gate) * up) @ w_down
    x = layer_norm(ALPHA * x + g2 * f, ln2_g, ln2_b)
    return x, s_f, s_b


def setup_inputs(seed: int = 0) -> dict:
    key = jax.random.key(seed)
    ks = jax.random.split(key, 22)
    nrm = lambda k, shape, s: jax.random.normal(k, shape, jnp.float32) * s
    return {
        'x_prompt': nrm(ks[0], (BATCH, SEQ, D_MODEL), 1.0),
        'x_sample': nrm(ks[1], (DEC_BATCH, DEC_SEQ, D_MODEL), 1.0),
        'state_gla': nrm(ks[2], (DEPTH, 2, DEC_BATCH, GLA_HEADS, GLA_DK, GLA_DV), 0.5),
        'c': nrm(ks[3], (DEC_BATCH, D_MODEL), 1.0),
        'c_ctx': nrm(ks[4], (D_MODEL,), 1.0),
        'ln_in_g': 1.0 + nrm(ks[5], (D_MODEL,), 0.05),
        'ln_in_b': nrm(ks[6], (D_MODEL,), 0.02),
        'w_ada': nrm(ks[7], (DEPTH, D_MODEL, 6 * D_MODEL), 0.5 * D_MODEL ** -0.5),
        'b_ada': nrm(ks[8], (DEPTH, 6 * D_MODEL), 0.02),
        'w_in': nrm(ks[9], (DEPTH, D_MODEL, IN_WIDTH), D_MODEL ** -0.5),
        'w_a2': nrm(ks[10], (DEPTH, 2, GLA_GATE_RANK, GLA_KEY_WIDTH), GLA_GATE_RANK ** -0.5),
        'b_a': nrm(ks[11], (DEPTH, 2, GLA_KEY_WIDTH), 0.5),
        'gla_norm_g': 1.0 + nrm(ks[12], (DEPTH, GLA_WIDTH), 0.05),
        'w_pool': nrm(ks[13], (DEPTH, POOL_GROUPS, POOL_GROUP_DIM, POOL_GROUP_DIM), POOL_GROUP_DIM ** -0.5),
        'pool_scale': 1.0 + nrm(ks[14], (DEPTH, POOL_WIDTH), 0.1),
        'w_out': nrm(ks[15], (DEPTH, MIX_WIDTH, D_MODEL), BETA * MIX_WIDTH ** -0.5),
        'ln1_g': 1.0 + nrm(ks[16], (DEPTH, D_MODEL), 0.05),
        'ln1_b': nrm(ks[17], (DEPTH, D_MODEL), 0.02),
        'w_gu': nrm(ks[18], (DEPTH, D_MODEL, 2 * D_FF), D_MODEL ** -0.5),
        'w_down': nrm(ks[19], (DEPTH, D_FF, D_MODEL), BETA * D_FF ** -0.5),
        'ln2_g': 1.0 + nrm(ks[20], (DEPTH, D_MODEL), 0.05),
        'ln2_b': nrm(ks[21], (DEPTH, D_MODEL), 0.02),
    }


def reference(x_prompt, x_sample, state_gla, c, c_ctx, ln_in_g, ln_in_b, w_ada, b_ada, w_in, w_a2,
              b_a, gla_norm_g, w_pool, pool_scale, w_out, ln1_g, ln1_b, w_gu, w_down, ln2_g, ln2_b):
    xp = layer_norm(x_prompt, ln_in_g, ln_in_b)
    xs = layer_norm(x_sample, ln_in_g, ln_in_b)
    zero_state = jnp.zeros((x_prompt.shape[0], GLA_HEADS, GLA_DK, GLA_DV), jnp.float32)
    states = []
    for l in range(DEPTH):
        lw = (w_in[l], w_a2[l], b_a[l], gla_norm_g[l], w_pool[l], pool_scale[l], w_out[l],
              ln1_g[l], ln1_b[l], w_gu[l], w_down[l], ln2_g[l], ln2_b[l])
        mod_ctx = jax.nn.silu(c_ctx) @ w_ada[l] + b_ada[l]
        mod_lat = (jax.nn.silu(c) @ w_ada[l] + b_ada[l])[:, None, :]
        xp, s_f, s_b = block(xp, mod_ctx, zero_state, zero_state, False, *lw)
        states.append(jnp.stack([s_f, s_b], axis=0))
        xs, _, _ = block(xs, mod_lat, state_gla[l, 0], state_gla[l, 1], True, *lw)
    y_prompt = xp
    y_sample = xs
    new_state_gla = jnp.stack(states, axis=0).astype(x_prompt.dtype)
    return (y_prompt, y_sample, new_state_gla)
```

```python
import functools

import numpy as np
import jax
import jax.numpy as jnp
from jax import lax
from jax.experimental import pallas as pl
from jax.experimental.pallas import tpu as pltpu

F32 = jnp.float32
BF16 = jnp.bfloat16

D_MODEL = 2048
DEPTH = 4
GLA_HEADS = 4
GLA_DV = 256
GLA_DK = 128
GLA_KEY_WIDTH = GLA_HEADS * GLA_DK
GLA_WIDTH = GLA_HEADS * GLA_DV
POOL_WIDTH = 1024
GLA_GATE_RANK = 16
GLA_GATE_TEMP = 16.0
GLA_CHUNK = 64
POOL_WINDOWS = (2, 4, 8, 16)
POOL_GROUP_DIM = 256
GRID_W = 64
D_FF = 5632
ALPHA = (2 * DEPTH) ** 0.25
LN_EPS = 1e-5
Q_SCALE = GLA_DK ** -0.5

Z_Q = 0
Z_K = Z_Q + GLA_KEY_WIDTH
Z_V = Z_K + GLA_KEY_WIDTH
Z_G = Z_V + GLA_WIDTH
Z_P = Z_G + GLA_WIDTH
Z_GATE = Z_P + POOL_WIDTH
Z_GATE_PAD = 128
Z_WIDTH = Z_GATE + Z_GATE_PAD
W_AF = 2 * GLA_KEY_WIDTH + 2 * GLA_WIDTH
W_P = W_AF + 2 * GLA_GATE_RANK

PAIR = 2 * GLA_CHUNK
P1_BLK = 256
POOL_BLK = 256
V7X_VMEM_LIMIT_BYTES = 56 * 2 ** 20


CHUNK_SHIFT = GLA_CHUNK.bit_length() - 1
GRID_SHIFT = GRID_W.bit_length() - 1


def _log_sigmoid(x):
    return jnp.minimum(x, 0.0) - jnp.log1p(jnp.exp(-jnp.abs(x)))


def _dot(a, b):
    return jnp.dot(a, b, preferred_element_type=F32)


def _dot_nt(a, b):
    return lax.dot_general(a, b, (((1,), (1,)), ((), ())), preferred_element_type=F32)


def _cparams(semantics):
    return pltpu.CompilerParams(dimension_semantics=semantics, vmem_limit_bytes=V7X_VMEM_LIMIT_BYTES)


def _layer_norm(y, g, b):
    mu = jnp.mean(y, axis=-1, keepdims=True)
    yc = y - mu
    var = jnp.mean(yc * yc, axis=-1, keepdims=True)
    return yc * lax.rsqrt(var + LN_EPS) * g + b


def _mod_spec(part, rows_per_mod, tm):
    if rows_per_mod is None:
        return pl.BlockSpec((None, 1, D_MODEL), lambda i, *_: (0, 0, part))
    return pl.BlockSpec((None, 1, D_MODEL), lambda i, *_: (1 + (i * tm) // rows_per_mod, 0, part))


def _ada_kernel(c_ref, w_ref, b_ref, o_ref):
    c = c_ref[...]
    s = jax.nn.silu(c).astype(BF16)
    o_ref[...] = _dot(s, w_ref[...].astype(BF16)) + b_ref[...]


def _ada_call(c_all, w_ada, b_ada):
    rows = c_all.shape[0]
    tn = 1024
    nj = (6 * D_MODEL) // tn
    return pl.pallas_call(
        _ada_kernel,
        grid=(DEPTH, nj),
        in_specs=[
            pl.BlockSpec((rows, D_MODEL), lambda l, j: (0, 0)),
            pl.BlockSpec((None, D_MODEL, tn), lambda l, j: (l, 0, j)),
            pl.BlockSpec((None, 1, tn), lambda l, j: (l, 0, j)),
        ],
        out_specs=pl.BlockSpec((None, rows, tn), lambda l, j: (l, 0, j)),
        out_shape=jax.ShapeDtypeStruct((DEPTH, rows, 6 * D_MODEL), F32),
        compiler_params=_cparams(("arbitrary", "arbitrary")),
        name="ada_mod",
    )(c_all, w_ada, b_ada.reshape(DEPTH, 1, 6 * D_MODEL))


def _ln_in_kernel(x_ref, g_ref, b_ref, sh_ref, sc_ref, x0_ref, h_ref):
    y = _layer_norm(x_ref[...], g_ref[...], b_ref[...])
    x0_ref[...] = y
    h_ref[...] = (y * (1.0 + sc_ref[...]) + sh_ref[...]).astype(BF16)


def _ln_in_call(x, g, b, mod3, rows_per_mod):
    m = x.shape[0]
    tm = min(512, m)
    row = pl.BlockSpec((tm, D_MODEL), lambda i: (i, 0))
    vec = pl.BlockSpec((1, D_MODEL), lambda i: (0, 0))
    return pl.pallas_call(
        _ln_in_kernel,
        grid=(m // tm,),
        in_specs=[row, vec, vec, _mod_spec(0, rows_per_mod, tm), _mod_spec(1, rows_per_mod, tm)],
        out_specs=[row, row],
        out_shape=[jax.ShapeDtypeStruct((m, D_MODEL), F32), jax.ShapeDtypeStruct((m, D_MODEL), BF16)],
        compiler_params=_cparams(("arbitrary",)),
        name="ln_in",
    )(x, g.reshape(1, D_MODEL), b.reshape(1, D_MODEL), mod3, mod3)


def _mm_kernel(h_ref, w_ref, o_ref):
    o_ref[...] = _dot(h_ref[...], w_ref[...])


def _mm_in_call(h, w):
    m = h.shape[0]
    tm = min(1024, m)
    tn = Z_WIDTH // 3
    return pl.pallas_call(
        _mm_kernel,
        grid=(Z_WIDTH // tn, m // tm),
        in_specs=[
            pl.BlockSpec((tm, D_MODEL), lambda j, i: (i, 0)),
            pl.BlockSpec((D_MODEL, tn), lambda j, i: (0, j)),
        ],
        out_specs=pl.BlockSpec((tm, tn), lambda j, i: (i, j)),
        out_shape=jax.ShapeDtypeStruct((m, Z_WIDTH), F32),
        compiler_params=_cparams(("arbitrary", "arbitrary")),
        name="mm_in",
    )(h, w)


def _gla_kernel(*refs, seq_len, has_s0, emit_state):
    (q_ref, k_ref, v_ref, g_ref, zg_ref, wgf_ref, wgb_ref, bgf_ref, bgb_ref, gn_ref,
     tl_ref, on_ref) = refs[:12]
    pos = 12
    if has_s0:
        s0f_ref, s0b_ref = refs[pos:pos + 2]
        pos += 2
    mo_ref = refs[pos]
    pos += 1
    if emit_state:
        st_ref = refs[pos]
        pos += 1
    (qef, kef, kdf_e, kdf_o, qeb, keb, kdb_e, kdb_o, vbf, vt_ref, of_ref, decf, decb) = refs[pos:]

    n_blk = seq_len // P1_BLK
    n_pairs = seq_len // PAIR

    blk_row = lax.broadcasted_iota(jnp.int32, (P1_BLK, GLA_DK), 0)
    even_chunk = ((blk_row >> CHUNK_SHIFT) & 1) == 0

    def phase1(i, carry):
        r = pl.multiple_of(i * P1_BLK, P1_BLK)
        rows = pl.ds(r, P1_BLK)
        zg = zg_ref[rows, :].astype(BF16)
        la_f = _log_sigmoid(_dot(zg, wgf_ref[...]) + bgf_ref[...]) * (1.0 / GLA_GATE_TEMP)
        la_b = _log_sigmoid(_dot(zg, wgb_ref[...]) + bgb_ref[...]) * (1.0 / GLA_GATE_TEMP)
        la = jnp.concatenate([la_f, la_b], axis=-1)
        hi = la.astype(BF16)
        r1 = la - hi.astype(F32)
        mid = r1.astype(BF16)
        lo = (r1 - mid.astype(F32)).astype(BF16)
        tl = tl_ref[...]
        on = on_ref[...]
        pre = _dot(tl, hi) + _dot(tl, mid) + _dot(tl, lo)
        tot = _dot(on, hi) + _dot(on, mid) + _dot(on, lo)
        b_f = pre[:, :GLA_DK]
        tot_f = tot[:, :GLA_DK]
        tot_b = tot[:, GLA_DK:]
        b_b = tot_b - pre[:, GLA_DK:] + la_b
        q = q_ref[rows, :] * Q_SCALE
        k = k_ref[rows, :]
        qef[rows, :] = (q * jnp.exp(b_f)).astype(BF16)
        kef[rows, :] = (k * jnp.exp(-b_f)).astype(BF16)
        kd = k * jnp.exp(tot_f - b_f)
        kdf_e[rows, :] = jnp.where(even_chunk, kd, 0.0).astype(BF16)
        kdf_o[rows, :] = jnp.where(even_chunk, 0.0, kd).astype(BF16)
        decf[rows, :] = jnp.exp(tot_f)
        qeb[rows, :] = (q * jnp.exp(b_b)).astype(BF16)
        keb[rows, :] = (k * jnp.exp(-b_b)).astype(BF16)
        kd = k * jnp.exp(tot_b - b_b)
        kdb_e[rows, :] = jnp.where(even_chunk, kd, 0.0).astype(BF16)
        kdb_o[rows, :] = jnp.where(even_chunk, 0.0, kd).astype(BF16)
        decb[rows, :] = jnp.exp(tot_b)
        v = v_ref[rows, :]
        vbf[rows, :] = v.astype(BF16)
        vt = v.T
        vt_ref[2 * i] = vt[:, :PAIR].astype(BF16)
        vt_ref[2 * i + 1] = vt[:, PAIR:].astype(BF16)
        return carry

    lax.fori_loop(0, n_blk, phase1, 0)

    prow = lax.broadcasted_iota(jnp.int32, (PAIR, PAIR), 0)
    pcol = lax.broadcasted_iota(jnp.int32, (PAIR, PAIR), 1)
    same_chunk = (prow >> CHUNK_SHIFT) == (pcol >> CHUNK_SHIFT)
    mask_f = jnp.logical_and(same_chunk, pcol <= prow)
    mask_b = jnp.logical_and(same_chunk, pcol >= prow)
    c = GLA_CHUNK

    def fwd_pair(p, st):
        r = pl.multiple_of(p * PAIR, PAIR)
        rows = pl.ds(r, PAIR)
        qe = qef[rows, :]
        vt = vt_ref[p]
        a = jnp.where(mask_f, _dot_nt(qe, kef[rows, :]), 0.0).astype(BF16)
        o = _dot(a, vbf[rows, :])
        r1 = pl.multiple_of(r + c, c)
        o0 = o[:c] + _dot_nt(qe[:c], st.astype(BF16))
        st = st * decf[pl.ds(r, 1), :] + _dot(vt, kdf_e[rows, :])
        o1 = o[c:] + _dot_nt(qe[c:], st.astype(BF16))
        st = st * decf[pl.ds(r1, 1), :] + _dot(vt, kdf_o[rows, :])
        of_ref[pl.ds(r, c), :] = o0
        of_ref[pl.ds(r1, c), :] = o1
        return st

    def finalize(o, rows):
        ms = jnp.mean(o * o, axis=-1, keepdims=True)
        y = o * lax.rsqrt(ms + LN_EPS) * gn_ref[...]
        mo_ref[rows, :] = (y * jax.nn.silu(g_ref[rows, :])).astype(BF16)

    def bwd_pair(j, st):
        p = n_pairs - 1 - j
        r = pl.multiple_of(p * PAIR, PAIR)
        rows = pl.ds(r, PAIR)
        qe = qeb[rows, :]
        vt = vt_ref[p]
        a = jnp.where(mask_b, _dot_nt(qe, keb[rows, :]), 0.0).astype(BF16)
        o = _dot(a, vbf[rows, :])
        r1 = pl.multiple_of(r + c, c)
        o1 = o[c:] + _dot_nt(qe[c:], st.astype(BF16))
        st = st * decb[pl.ds(r1, 1), :] + _dot(vt, kdb_o[rows, :])
        o0 = o[:c] + _dot_nt(qe[:c], st.astype(BF16))
        st = st * decb[pl.ds(r, 1), :] + _dot(vt, kdb_e[rows, :])
        finalize(o0 + of_ref[pl.ds(r, c), :], pl.ds(r, c))
        finalize(o1 + of_ref[pl.ds(r1, c), :], pl.ds(r1, c))
        return st

    if has_s0:
        st0_f = s0f_ref[...].T
        st0_b = s0b_ref[...].T
    else:
        st0_f = jnp.zeros((GLA_DV, GLA_DK), F32)
        st0_b = st0_f
    unroll = True if n_pairs <= 2 else 2
    st_f = lax.fori_loop(0, n_pairs, fwd_pair, st0_f, unroll=unroll)
    st_b = lax.fori_loop(0, n_pairs, bwd_pair, st0_b, unroll=unroll)
    if emit_state:
        st_ref[0] = st_f.T
        st_ref[1] = st_b.T


def _chunk_sum_matrices():
    i = np.arange(P1_BLK)
    same = (i[:, None] // GLA_CHUNK) == (i[None, :] // GLA_CHUNK)
    tl = np.logical_and(same, i[None, :] <= i[:, None]).astype(np.float32)
    return jnp.asarray(tl, BF16), jnp.asarray(same.astype(np.float32), BF16)


def _gla_call(z, wgate, bgate, gnorm, s0, nseq, seq_len, emit_state):
    m = nseq * seq_len
    tl, on = _chunk_sum_matrices()
    has_s0 = s0 is not None
    kw, vw = GLA_DK, GLA_DV
    in_specs = [
        pl.BlockSpec((seq_len, kw), lambda s, h: (s, Z_Q // kw + h)),
        pl.BlockSpec((seq_len, kw), lambda s, h: (s, Z_K // kw + h)),
        pl.BlockSpec((seq_len, vw), lambda s, h: (s, Z_V // vw + h)),
        pl.BlockSpec((seq_len, vw), lambda s, h: (s, Z_G // vw + h)),
        pl.BlockSpec((seq_len, Z_GATE_PAD), lambda s, h: (s, Z_GATE // Z_GATE_PAD)),
        pl.BlockSpec((Z_GATE_PAD, kw), lambda s, h: (0, h)),
        pl.BlockSpec((Z_GATE_PAD, kw), lambda s, h: (0, GLA_HEADS + h)),
        pl.BlockSpec((1, kw), lambda s, h: (0, h)),
        pl.BlockSpec((1, kw), lambda s, h: (0, GLA_HEADS + h)),
        pl.BlockSpec((1, vw), lambda s, h: (0, h)),
        pl.BlockSpec((P1_BLK, P1_BLK), lambda s, h: (0, 0)),
        pl.BlockSpec((P1_BLK, P1_BLK), lambda s, h: (0, 0)),
    ]
    args = [z, z, z, z, z, wgate, wgate, bgate, bgate, gnorm, tl, on]
    if has_s0:
        in_specs += [
            pl.BlockSpec((None, None, None, kw, vw), lambda s, h: (0, s, h, 0, 0)),
            pl.BlockSpec((None, None, None, kw, vw), lambda s, h: (1, s, h, 0, 0)),
        ]
        args += [s0, s0]
    out_specs = [pl.BlockSpec((seq_len, vw), lambda s, h: (s, h))]
    out_shape = [jax.ShapeDtypeStruct((m, GLA_WIDTH), BF16)]
    if emit_state:
        out_specs.append(pl.BlockSpec((2, None, None, kw, vw), lambda s, h: (0, s, h, 0, 0)))
        out_shape.append(jax.ShapeDtypeStruct((2, nseq, GLA_HEADS, kw, vw), F32))
    bf_rows = lambda: pltpu.VMEM((seq_len, kw), BF16)
    scratch = [bf_rows() for _ in range(8)] + [
        pltpu.VMEM((seq_len, vw), BF16),
        pltpu.VMEM((seq_len // PAIR, vw, PAIR), BF16),
        pltpu.VMEM((seq_len, vw), F32),
        pltpu.VMEM((seq_len, kw), F32),
        pltpu.VMEM((seq_len, kw), F32),
    ]
    return pl.pallas_call(
        functools.partial(_gla_kernel, seq_len=seq_len, has_s0=has_s0, emit_state=emit_state),
        grid=(nseq, GLA_HEADS),
        in_specs=in_specs,
        out_specs=out_specs,
        out_shape=out_shape,
        scratch_shapes=scratch,
        compiler_params=_cparams(("arbitrary", "arbitrary")),
        name="gla_state" if emit_state else "gla",
    )(*args)


def _window_count(idx, n, w):
    lo = jnp.maximum(idx - w // 2, 0)
    hi = jnp.minimum(idx - w // 2 + w, n)
    return hi - lo


def _pool_project(pooled, gi, w_ref, sc_ref, o_ref, rows):
    cols = slice(gi * POOL_GROUP_DIM, (gi + 1) * POOL_GROUP_DIM)
    y = _dot(pooled.astype(BF16), w_ref[gi]) * sc_ref[:, cols]
    o_ref[rows, cols] = y.astype(BF16)


def _pool1d_kernel(p_ref, w_ref, sc_ref, o_ref, *, seq_len):
    t = lax.broadcasted_iota(jnp.int32, (seq_len, 1), 0)
    for gi, w in enumerate(POOL_WINDOWS):
        x = p_ref[:, gi * POOL_GROUP_DIM:(gi + 1) * POOL_GROUP_DIM]
        acc = jnp.zeros_like(x)
        for j in range(w):
            d = j - w // 2
            valid = jnp.logical_and(t + d >= 0, t + d < seq_len)
            term = x if d == 0 else pltpu.roll(x, (-d) % seq_len, 0)
            acc = acc + jnp.where(valid, term, 0.0)
        cnt = _window_count(t, seq_len, w).astype(F32)
        _pool_project(acc / cnt - x, gi, w_ref, sc_ref, o_ref, slice(None))


def _pool2d_kernel(p_ref, w_ref, sc_ref, o_ref, cs_ref, *, seq_len):
    n_rows = seq_len // GRID_W
    pad = (max(POOL_WINDOWS) // 2) * GRID_W
    n_blk = seq_len // POOL_BLK
    zeros = jnp.zeros((pad, POOL_GROUP_DIM), F32)
    cs_ref[pl.ds(0, pad), :] = zeros
    cs_ref[pl.ds(pad + seq_len, pad), :] = zeros
    local = lax.broadcasted_iota(jnp.int32, (POOL_BLK, 1), 0)
    col = local & (GRID_W - 1)
    for gi, w in enumerate(POOL_WINDOWS):
        cols = slice(gi * POOL_GROUP_DIM, (gi + 1) * POOL_GROUP_DIM)

        def col_sums(i, carry, w=w, cols=cols):
            r = pl.multiple_of(i * POOL_BLK, POOL_BLK)
            x = p_ref[pl.ds(r, POOL_BLK), cols]
            acc = jnp.zeros_like(x)
            for j in range(w):
                d = j - w // 2
                valid = jnp.logical_and(col + d >= 0, col + d < GRID_W)
                term = x if d == 0 else pltpu.roll(x, (-d) % POOL_BLK, 0)
                acc = acc + jnp.where(valid, term, 0.0)
            cs_ref[pl.ds(pl.multiple_of(r + pad, GRID_W), POOL_BLK), :] = acc
            return carry

        lax.fori_loop(0, n_blk, col_sums, 0)

        def row_sums(i, carry, w=w, cols=cols, gi=gi):
            r = pl.multiple_of(i * POOL_BLK, POOL_BLK)
            acc = jnp.zeros((POOL_BLK, POOL_GROUP_DIM), F32)
            for j in range(w):
                off = pad + (j - w // 2) * GRID_W
                acc = acc + cs_ref[pl.ds(pl.multiple_of(r + off, GRID_W), POOL_BLK), :]
            grow = (r + local) >> GRID_SHIFT
            cnt = (_window_count(grow, n_rows, w) * _window_count(col, GRID_W, w)).astype(F32)
            x = p_ref[pl.ds(r, POOL_BLK), cols]
            _pool_project(acc / cnt - x, gi, w_ref, sc_ref, o_ref, pl.ds(r, POOL_BLK))
            return carry

        lax.fori_loop(0, n_blk, row_sums, 0)


def _pool_call(z, w_pool, pool_scale, nseq, seq_len, on_grid):
    m = nseq * seq_len
    if on_grid:
        body = functools.partial(_pool2d_kernel, seq_len=seq_len)
        pad = (max(POOL_WINDOWS) // 2) * GRID_W
        scratch = [pltpu.VMEM((seq_len + 2 * pad, POOL_GROUP_DIM), F32)]
    else:
        body = functools.partial(_pool1d_kernel, seq_len=seq_len)
        scratch = []
    return pl.pallas_call(
        body,
        grid=(nseq,),
        in_specs=[
            pl.BlockSpec((seq_len, POOL_WIDTH), lambda s: (s, Z_P // POOL_WIDTH)),
            pl.BlockSpec((len(POOL_WINDOWS), POOL_GROUP_DIM, POOL_GROUP_DIM), lambda s: (0, 0, 0)),
            pl.BlockSpec((1, POOL_WIDTH), lambda s: (0, 0)),
        ],
        out_specs=pl.BlockSpec((seq_len, POOL_WIDTH), lambda s: (s, 0)),
        out_shape=jax.ShapeDtypeStruct((m, POOL_WIDTH), BF16),
        scratch_shapes=scratch,
        compiler_params=_cparams(("arbitrary",)),
        name="pool2d" if on_grid else "pool1d",
    )(z, w_pool, pool_scale)


def _mm_out_ln_kernel(mo_ref, mp_ref, w_ref, x_ref, g1_ref, sh2_ref, sc2_ref, lg_ref, lb_ref,
                      x1_ref, h2_ref):
    m = _dot(mo_ref[...], w_ref[:GLA_WIDTH, :]) + _dot(mp_ref[...], w_ref[GLA_WIDTH:, :])
    x1 = _layer_norm(ALPHA * x_ref[...] + g1_ref[...] * m, lg_ref[...], lb_ref[...])
    x1_ref[...] = x1
    h2_ref[...] = (x1 * (1.0 + sc2_ref[...]) + sh2_ref[...]).astype(BF16)


def _mm_out_ln_call(mo, mp, w_out, x, mod3, rows_per_mod, ln_g, ln_b):
    m = x.shape[0]
    tm = min(512, m)
    row = pl.BlockSpec((tm, D_MODEL), lambda i: (i, 0))
    half = pl.BlockSpec((tm, GLA_WIDTH), lambda i: (i, 0))
    vec = pl.BlockSpec((1, D_MODEL), lambda i: (0, 0))
    return pl.pallas_call(
        _mm_out_ln_kernel,
        grid=(m // tm,),
        in_specs=[half, half, pl.BlockSpec((D_MODEL, D_MODEL), lambda i: (0, 0)), row,
                  _mod_spec(2, rows_per_mod, tm), _mod_spec(3, rows_per_mod, tm),
                  _mod_spec(4, rows_per_mod, tm), vec, vec],
        out_specs=[row, row],
        out_shape=[jax.ShapeDtypeStruct((m, D_MODEL), F32), jax.ShapeDtypeStruct((m, D_MODEL), BF16)],
        compiler_params=_cparams(("arbitrary",)),
        name="mm_out_ln",
    )(mo, mp, w_out, x, mod3, mod3, mod3, ln_g.reshape(1, D_MODEL), ln_b.reshape(1, D_MODEL))


def _mm_gu_kernel(h_ref, wg_ref, wu_ref, o_ref):
    h = h_ref[...]
    gate = _dot(h, wg_ref[...])
    up = _dot(h, wu_ref[...])
    o_ref[...] = (jax.nn.silu(gate) * up).astype(BF16)


def _mm_gu_call(h, w_gu):
    m = h.shape[0]
    tm = min(1024, m)
    tn = 512
    nj = D_FF // tn
    return pl.pallas_call(
        _mm_gu_kernel,
        grid=(nj, m // tm),
        in_specs=[
            pl.BlockSpec((tm, D_MODEL), lambda j, i: (i, 0)),
            pl.BlockSpec((D_MODEL, tn), lambda j, i: (0, j)),
            pl.BlockSpec((D_MODEL, tn), lambda j, i: (0, nj + j)),
        ],
        out_specs=pl.BlockSpec((tm, tn), lambda j, i: (i, j)),
        out_shape=jax.ShapeDtypeStruct((m, D_FF), BF16),
        compiler_params=_cparams(("arbitrary", "arbitrary")),
        name="mm_gu",
    )(h, w_gu, w_gu)


def _mm_down_ln_kernel(*refs, nk, emit_h):
    a_ref, w_ref, x1_ref, g2_ref = refs[:4]
    pos = 4
    if emit_h:
        shn_ref, scn_ref = refs[pos:pos + 2]
        pos += 2
    lg_ref, lb_ref, x_ref = refs[pos:pos + 3]
    pos += 3
    if emit_h:
        h_ref = refs[pos]
        pos += 1
    acc_ref = refs[pos]
    k = pl.program_id(1)
    prod = _dot(a_ref[...], w_ref[...])

    @pl.when(k == 0)
    def _():
        acc_ref[...] = prod

    @pl.when(k != 0)
    def _():
        acc_ref[...] += prod

    @pl.when(k == nk - 1)
    def _():
        x = _layer_norm(ALPHA * x1_ref[...] + g2_ref[...] * acc_ref[...], lg_ref[...], lb_ref[...])
        x_ref[...] = x
        if emit_h:
            h_ref[...] = (x * (1.0 + scn_ref[...]) + shn_ref[...]).astype(BF16)


def _mm_down_ln_call(act, w_down, x1, mod3, mod3_next, rows_per_mod, ln_g, ln_b):
    m = x1.shape[0]
    tm = min(512, m)
    tk = D_FF // 4
    nk = D_FF // tk
    emit_h = mod3_next is not None
    row = pl.BlockSpec((tm, D_MODEL), lambda i, k: (i, 0))
    vec = pl.BlockSpec((1, D_MODEL), lambda i, k: (0, 0))
    in_specs = [
        pl.BlockSpec((tm, tk), lambda i, k: (i, k)),
        pl.BlockSpec((tk, D_MODEL), lambda i, k: (k, 0)),
        row,
        _mod_spec(5, rows_per_mod, tm),
    ]
    args = [act, w_down, x1, mod3]
    if emit_h:
        in_specs += [_mod_spec(0, rows_per_mod, tm), _mod_spec(1, rows_per_mod, tm)]
        args += [mod3_next, mod3_next]
    in_specs += [vec, vec]
    args += [ln_g.reshape(1, D_MODEL), ln_b.reshape(1, D_MODEL)]
    out_specs = [row]
    out_shape = [jax.ShapeDtypeStruct((m, D_MODEL), F32)]
    if emit_h:
        out_specs.append(row)
        out_shape.append(jax.ShapeDtypeStruct((m, D_MODEL), BF16))
    outs = pl.pallas_call(
        functools.partial(_mm_down_ln_kernel, nk=nk, emit_h=emit_h),
        grid=(m // tm, nk),
        in_specs=in_specs,
        out_specs=out_specs,
        out_shape=out_shape,
        scratch_shapes=[pltpu.VMEM((tm, D_MODEL), F32)],
        compiler_params=_cparams(("arbitrary", "arbitrary")),
        name="mm_down_ln",
    )(*args)
    return (outs[0], outs[1]) if emit_h else (outs[0], None)


def _prep_weights(w_in, w_a2, b_a, w_pool, w_out, w_gu, w_down):
    depth = w_in.shape[0]
    gate_pad = jnp.zeros((depth, D_MODEL, Z_GATE_PAD - 2 * GLA_GATE_RANK), w_in.dtype)
    w_in_r = jnp.concatenate(
        [w_in[:, :, :W_AF], w_in[:, :, W_P:], w_in[:, :, W_AF:W_P], gate_pad], axis=-1).astype(BF16)
    wgate = jnp.zeros((depth, Z_GATE_PAD, 2 * GLA_KEY_WIDTH), F32)
    wgate = wgate.at[:, :GLA_GATE_RANK, :GLA_KEY_WIDTH].set(w_a2[:, 0])
    wgate = wgate.at[:, GLA_GATE_RANK:2 * GLA_GATE_RANK, GLA_KEY_WIDTH:].set(w_a2[:, 1])
    bgate = b_a.reshape(depth, 1, 2 * GLA_KEY_WIDTH)
    return (w_in_r, wgate.astype(BF16), bgate, w_pool.astype(BF16), w_out.astype(BF16),
            w_gu.astype(BF16), w_down.astype(BF16))


def kernel(x_prompt, x_sample, state_gla, c, c_ctx, ln_in_g, ln_in_b, w_ada, b_ada, w_in, w_a2, b_a,
           gla_norm_g, w_pool, pool_scale, w_out, ln1_g, ln1_b, w_gu, w_down, ln2_g, ln2_b):
    nb_p, len_p, _ = x_prompt.shape
    nb_s, len_s, _ = x_sample.shape
    w_in_r, wgate, bgate, w_pool_b, w_out_b, w_gu_b, w_down_b = _prep_weights(
        w_in, w_a2, b_a, w_pool, w_out, w_gu, w_down)

    n_mod = -(-(1 + nb_s) // 8) * 8
    c_all = jnp.concatenate(
        [c_ctx[None, :], c, jnp.zeros((n_mod - 1 - nb_s, D_MODEL), c.dtype)], axis=0)
    mod_all = _ada_call(c_all, w_ada, b_ada)
    mod3 = [mod_all[l].reshape(n_mod, 1, 6 * D_MODEL) for l in range(DEPTH)]

    streams = [
        (x_prompt.reshape(nb_p * len_p, D_MODEL), None, nb_p, len_p, False),
        (x_sample.reshape(nb_s * len_s, D_MODEL), len_s, nb_s, len_s, True),
    ]
    results = []
    states = []
    for x_in, rows_per_mod, nseq, seq_len, on_grid in streams:
        x, h = _ln_in_call(x_in, ln_in_g, ln_in_b, mod3[0], rows_per_mod)
        for l in range(DEPTH):
            z = _mm_in_call(h, w_in_r[l])
            s0 = state_gla[l] if on_grid else None
            gla_out = _gla_call(z, wgate[l], bgate[l], gla_norm_g[l].reshape(1, GLA_WIDTH), s0,
                                nseq, seq_len, emit_state=not on_grid)
            if on_grid:
                mo = gla_out[0]
            else:
                mo, st = gla_out
                states.append(st)
            mp = _pool_call(z, w_pool_b[l], pool_scale[l].reshape(1, POOL_WIDTH), nseq, seq_len, on_grid)
            x1, h2 = _mm_out_ln_call(mo, mp, w_out_b[l], x, mod3[l], rows_per_mod, ln1_g[l], ln1_b[l])
            act = _mm_gu_call(h2, w_gu_b[l])
            mod3_next = mod3[l + 1] if l + 1 < DEPTH else None
            x, h = _mm_down_ln_call(act, w_down_b[l], x1, mod3[l], mod3_next, rows_per_mod,
                                    ln2_g[l], ln2_b[l])
        results.append(x)
    y_prompt = results[0].reshape(nb_p, len_p, D_MODEL)
    y_sample = results[1].reshape(nb_s, len_s, D_MODEL)
    new_state_gla = jnp.stack(states, axis=0).astype(x_prompt.dtype)
    return (y_prompt, y_sample, new_state_gla)
```

```python
import functools

import numpy as np
import jax
import jax.numpy as jnp
from jax import lax
from jax.experimental import pallas as pl
from jax.experimental.pallas import tpu as pltpu

F32 = jnp.float32
BF16 = jnp.bfloat16

D_MODEL = 2048
DEPTH = 4
GLA_HEADS = 4
GLA_DV = 256
GLA_DK = 128
GLA_KEY_WIDTH = GLA_HEADS * GLA_DK
GLA_WIDTH = GLA_HEADS * GLA_DV
POOL_WIDTH = 1024
GLA_GATE_RANK = 16
GLA_GATE_TEMP = 16.0
GLA_CHUNK = 64
POOL_WINDOWS = (2, 4, 8, 16)
POOL_GROUP_DIM = 256
GRID_W = 64
D_FF = 5632
ALPHA = (2 * DEPTH) ** 0.25
LN_EPS = 1e-5
Q_SCALE = GLA_DK ** -0.5

Z_Q = 0
Z_K = Z_Q + GLA_KEY_WIDTH
Z_V = Z_K + GLA_KEY_WIDTH
Z_G = Z_V + GLA_WIDTH
Z_P = Z_G + GLA_WIDTH
Z_GATE = Z_P + POOL_WIDTH
Z_GATE_PAD = 128
Z_WIDTH = Z_GATE + Z_GATE_PAD
W_AF = 2 * GLA_KEY_WIDTH + 2 * GLA_WIDTH
W_P = W_AF + 2 * GLA_GATE_RANK

PAIR = 2 * GLA_CHUNK
P1_BLK = 256
POOL_BLK = 256
LN_SUB = 128
CHUNK_SHIFT = GLA_CHUNK.bit_length() - 1
GRID_SHIFT = GRID_W.bit_length() - 1
V7X_VMEM_LIMIT_BYTES = 56 * 2 ** 20


def _log_sigmoid(x):
    return jnp.minimum(x, 0.0) - jnp.log1p(jnp.exp(-jnp.abs(x)))


def _dot(a, b):
    return jnp.dot(a, b, preferred_element_type=F32)


def _dot_nt(a, b):
    return lax.dot_general(a, b, (((1,), (1,)), ((), ())), preferred_element_type=F32)


def _cparams(semantics):
    return pltpu.CompilerParams(dimension_semantics=semantics, vmem_limit_bytes=V7X_VMEM_LIMIT_BYTES)


def _layer_norm(y, g, b):
    mu = jnp.mean(y, axis=-1, keepdims=True)
    yc = y - mu
    var = jnp.mean(yc * yc, axis=-1, keepdims=True)
    return yc * lax.rsqrt(var + LN_EPS) * g + b


def _mod_spec(layer, part, rows_per_mod, tm):
    if rows_per_mod is None:
        return pl.BlockSpec((None, None, 1, D_MODEL), lambda i, *_: (layer, 0, 0, part))
    return pl.BlockSpec((None, None, 1, D_MODEL),
                        lambda i, *_: (layer, 1 + (i * tm) // rows_per_mod, 0, part))


def _layer_vec_spec(layer, width):
    return pl.BlockSpec((None, 1, width), lambda *_: (layer, 0, 0))


def _ada_kernel(c_ref, w_ref, b_ref, o_ref):
    c = c_ref[...]
    s = jax.nn.silu(c).astype(BF16)
    o_ref[...] = _dot(s, w_ref[...].astype(BF16)) + b_ref[...]


def _ada_call(c_all, w_ada, b_ada):
    rows = c_all.shape[0]
    tn = 1024
    nj = (6 * D_MODEL) // tn
    return pl.pallas_call(
        _ada_kernel,
        grid=(DEPTH, nj),
        in_specs=[
            pl.BlockSpec((rows, D_MODEL), lambda l, j: (0, 0)),
            pl.BlockSpec((None, D_MODEL, tn), lambda l, j: (l, 0, j)),
            pl.BlockSpec((None, 1, tn), lambda l, j: (l, 0, j)),
        ],
        out_specs=pl.BlockSpec((None, rows, tn), lambda l, j: (l, 0, j)),
        out_shape=jax.ShapeDtypeStruct((DEPTH, rows, 6 * D_MODEL), F32),
        compiler_params=_cparams(("arbitrary", "arbitrary")),
        name="ada_mod",
    )(c_all, w_ada, b_ada.reshape(DEPTH, 1, 6 * D_MODEL))


def _ln_in_kernel(x_ref, g_ref, b_ref, sh_ref, sc_ref, x0_ref, h_ref):
    y = _layer_norm(x_ref[...], g_ref[...], b_ref[...])
    x0_ref[...] = y
    h_ref[...] = (y * (1.0 + sc_ref[...]) + sh_ref[...]).astype(BF16)


def _ln_in_call(x, g, b, mod4, rows_per_mod):
    m = x.shape[0]
    tm = min(512, m)
    row = pl.BlockSpec((tm, D_MODEL), lambda i: (i, 0))
    vec = pl.BlockSpec((1, D_MODEL), lambda i: (0, 0))
    return pl.pallas_call(
        _ln_in_kernel,
        grid=(m // tm,),
        in_specs=[row, vec, vec, _mod_spec(0, 0, rows_per_mod, tm), _mod_spec(0, 1, rows_per_mod, tm)],
        out_specs=[row, row],
        out_shape=[jax.ShapeDtypeStruct((m, D_MODEL), F32), jax.ShapeDtypeStruct((m, D_MODEL), BF16)],
        compiler_params=_cparams(("arbitrary",)),
        name="ln_in",
    )(x, g.reshape(1, D_MODEL), b.reshape(1, D_MODEL), mod4, mod4)


def _mm_kernel(h_ref, w_ref, o_ref):
    o_ref[...] = _dot(h_ref[...], w_ref[...])


def _mm_in_call(h, w, layer):
    m = h.shape[0]
    tm = min(1024, m)
    tn = Z_WIDTH // 3
    return pl.pallas_call(
        _mm_kernel,
        grid=(Z_WIDTH // tn, m // tm),
        in_specs=[
            pl.BlockSpec((tm, D_MODEL), lambda j, i: (i, 0)),
            pl.BlockSpec((None, D_MODEL, tn), lambda j, i: (layer, 0, j)),
        ],
        out_specs=pl.BlockSpec((tm, tn), lambda j, i: (i, j)),
        out_shape=jax.ShapeDtypeStruct((m, Z_WIDTH), F32),
        compiler_params=_cparams(("arbitrary", "arbitrary")),
        name="mm_in",
    )(h, w)


def _gla_kernel(*refs, seq_len, has_s0, emit_state, alias_state):
    (q_ref, k_ref, v_ref, g_ref, zg_ref, wgf_ref, wgb_ref, bgf_ref, bgb_ref, gn_ref,
     tl_ref, on_ref) = refs[:12]
    pos = 12
    if has_s0:
        s0f_ref, s0b_ref = refs[pos:pos + 2]
        pos += 2
    if alias_state:
        pos += 1
    mo_ref = refs[pos]
    pos += 1
    if emit_state:
        st_ref = refs[pos]
        pos += 1
    (qef, kef, kdf_e, kdf_o, qeb, keb, kdb_e, kdb_o, vbf, vt_ref, of_ref, decf, decb) = refs[pos:]

    n_blk = seq_len // P1_BLK
    n_pairs = seq_len // PAIR

    blk_row = lax.broadcasted_iota(jnp.int32, (P1_BLK, GLA_DK), 0)
    even_chunk = ((blk_row >> CHUNK_SHIFT) & 1) == 0

    def phase1(i, carry):
        r = pl.multiple_of(i * P1_BLK, P1_BLK)
        rows = pl.ds(r, P1_BLK)
        zg = zg_ref[rows, :].astype(BF16)
        la_f = _log_sigmoid(_dot(zg, wgf_ref[...]) + bgf_ref[...]) * (1.0 / GLA_GATE_TEMP)
        la_b = _log_sigmoid(_dot(zg, wgb_ref[...]) + bgb_ref[...]) * (1.0 / GLA_GATE_TEMP)
        la = jnp.concatenate([la_f, la_b], axis=-1)
        hi = la.astype(BF16)
        r1 = la - hi.astype(F32)
        mid = r1.astype(BF16)
        lo = (r1 - mid.astype(F32)).astype(BF16)
        tl = tl_ref[...]
        on = on_ref[...]
        pre = _dot(tl, hi) + _dot(tl, mid) + _dot(tl, lo)
        tot = _dot(on, hi) + _dot(on, mid) + _dot(on, lo)
        b_f = pre[:, :GLA_DK]
        tot_f = tot[:, :GLA_DK]
        tot_b = tot[:, GLA_DK:]
        b_b = tot_b - pre[:, GLA_DK:] + la_b
        q = q_ref[rows, :] * Q_SCALE
        k = k_ref[rows, :]
        qef[rows, :] = (q * jnp.exp(b_f)).astype(BF16)
        kef[rows, :] = (k * jnp.exp(-b_f)).astype(BF16)
        kd = k * jnp.exp(tot_f - b_f)
        kdf_e[rows, :] = jnp.where(even_chunk, kd, 0.0).astype(BF16)
        kdf_o[rows, :] = jnp.where(even_chunk, 0.0, kd).astype(BF16)
        decf[rows, :] = jnp.exp(tot_f)
        qeb[rows, :] = (q * jnp.exp(b_b)).astype(BF16)
        keb[rows, :] = (k * jnp.exp(-b_b)).astype(BF16)
        kd = k * jnp.exp(tot_b - b_b)
        kdb_e[rows, :] = jnp.where(even_chunk, kd, 0.0).astype(BF16)
        kdb_o[rows, :] = jnp.where(even_chunk, 0.0, kd).astype(BF16)
        decb[rows, :] = jnp.exp(tot_b)
        v = v_ref[rows, :]
        vbf[rows, :] = v.astype(BF16)
        vt = v.T
        vt_ref[2 * i] = vt[:, :PAIR].astype(BF16)
        vt_ref[2 * i + 1] = vt[:, PAIR:].astype(BF16)
        return carry

    lax.fori_loop(0, n_blk, phase1, 0)

    prow = lax.broadcasted_iota(jnp.int32, (PAIR, PAIR), 0)
    pcol = lax.broadcasted_iota(jnp.int32, (PAIR, PAIR), 1)
    same_chunk = (prow >> CHUNK_SHIFT) == (pcol >> CHUNK_SHIFT)
    mask_f = jnp.logical_and(same_chunk, pcol <= prow)
    mask_b = jnp.logical_and(same_chunk, pcol >= prow)
    c = GLA_CHUNK

    def fwd_pair(p, st):
        r = pl.multiple_of(p * PAIR, PAIR)
        rows = pl.ds(r, PAIR)
        qe = qef[rows, :]
        vt = vt_ref[p]
        a = jnp.where(mask_f, _dot_nt(qe, kef[rows, :]), 0.0).astype(BF16)
        o = _dot(a, vbf[rows, :])
        r1 = pl.multiple_of(r + c, c)
        o0 = o[:c] + _dot_nt(qe[:c], st.astype(BF16))
        st = st * decf[pl.ds(r, 1), :] + _dot(vt, kdf_e[rows, :])
        o1 = o[c:] + _dot_nt(qe[c:], st.astype(BF16))
        st = st * decf[pl.ds(r1, 1), :] + _dot(vt, kdf_o[rows, :])
        of_ref[pl.ds(r, c), :] = o0
        of_ref[pl.ds(r1, c), :] = o1
        return st

    def finalize(o, rows):
        ms = jnp.mean(o * o, axis=-1, keepdims=True)
        y = o * lax.rsqrt(ms + LN_EPS) * gn_ref[...]
        mo_ref[rows, :] = (y * jax.nn.silu(g_ref[rows, :])).astype(BF16)

    def bwd_pair(j, st):
        p = n_pairs - 1 - j
        r = pl.multiple_of(p * PAIR, PAIR)
        rows = pl.ds(r, PAIR)
        qe = qeb[rows, :]
        vt = vt_ref[p]
        a = jnp.where(mask_b, _dot_nt(qe, keb[rows, :]), 0.0).astype(BF16)
        o = _dot(a, vbf[rows, :])
        r1 = pl.multiple_of(r + c, c)
        o1 = o[c:] + _dot_nt(qe[c:], st.astype(BF16))
        st = st * decb[pl.ds(r1, 1), :] + _dot(vt, kdb_o[rows, :])
        o0 = o[:c] + _dot_nt(qe[:c], st.astype(BF16))
        st = st * decb[pl.ds(r, 1), :] + _dot(vt, kdb_e[rows, :])
        finalize(o0 + of_ref[pl.ds(r, c), :], pl.ds(r, c))
        finalize(o1 + of_ref[pl.ds(r1, c), :], pl.ds(r1, c))
        return st

    if has_s0:
        st0_f = s0f_ref[...].T
        st0_b = s0b_ref[...].T
    else:
        st0_f = jnp.zeros((GLA_DV, GLA_DK), F32)
        st0_b = st0_f
    unroll = True if n_pairs <= 2 else 2
    st_f = lax.fori_loop(0, n_pairs, fwd_pair, st0_f, unroll=unroll)
    st_b = lax.fori_loop(0, n_pairs, bwd_pair, st0_b, unroll=unroll)
    if emit_state:
        st_ref[0] = st_f.T
        st_ref[1] = st_b.T


def _chunk_sum_matrices():
    i = np.arange(P1_BLK)
    same = (i[:, None] >> CHUNK_SHIFT) == (i[None, :] >> CHUNK_SHIFT)
    tl = np.logical_and(same, i[None, :] <= i[:, None]).astype(np.float32)
    return jnp.asarray(tl, BF16), jnp.asarray(same.astype(np.float32), BF16)


def _gla_call(z, wgate, bgate, gnorm, layer, nseq, seq_len, s0=None, emit_state=False, state_buf=None):
    m = nseq * seq_len
    tl, on = _chunk_sum_matrices()
    has_s0 = s0 is not None
    alias_state = state_buf is not None
    kw, vw = GLA_DK, GLA_DV
    in_specs = [
        pl.BlockSpec((seq_len, kw), lambda s, h: (s, Z_Q // kw + h)),
        pl.BlockSpec((seq_len, kw), lambda s, h: (s, Z_K // kw + h)),
        pl.BlockSpec((seq_len, vw), lambda s, h: (s, Z_V // vw + h)),
        pl.BlockSpec((seq_len, vw), lambda s, h: (s, Z_G // vw + h)),
        pl.BlockSpec((seq_len, Z_GATE_PAD), lambda s, h: (s, Z_GATE // Z_GATE_PAD)),
        pl.BlockSpec((None, Z_GATE_PAD, kw), lambda s, h: (layer, 0, h)),
        pl.BlockSpec((None, Z_GATE_PAD, kw), lambda s, h: (layer, 0, GLA_HEADS + h)),
        pl.BlockSpec((None, 1, kw), lambda s, h: (layer, 0, h)),
        pl.BlockSpec((None, 1, kw), lambda s, h: (layer, 0, GLA_HEADS + h)),
        pl.BlockSpec((None, 1, vw), lambda s, h: (layer, 0, h)),
        pl.BlockSpec((P1_BLK, P1_BLK), lambda s, h: (0, 0)),
        pl.BlockSpec((P1_BLK, P1_BLK), lambda s, h: (0, 0)),
    ]
    args = [z, z, z, z, z, wgate, wgate, bgate, bgate, gnorm, tl, on]
    if has_s0:
        in_specs += [
            pl.BlockSpec((None, None, None, None, kw, vw), lambda s, h: (layer, 0, s, h, 0, 0)),
            pl.BlockSpec((None, None, None, None, kw, vw), lambda s, h: (layer, 1, s, h, 0, 0)),
        ]
        args += [s0, s0]
    aliases = {}
    if alias_state:
        aliases = {len(args): 1}
        in_specs.append(pl.BlockSpec(memory_space=pl.ANY))
        args.append(state_buf)
    out_specs = [pl.BlockSpec((seq_len, vw), lambda s, h: (s, h))]
    out_shape = [jax.ShapeDtypeStruct((m, GLA_WIDTH), BF16)]
    if emit_state:
        out_specs.append(pl.BlockSpec((None, 2, None, None, kw, vw), lambda s, h: (layer, 0, s, h, 0, 0)))
        out_shape.append(jax.ShapeDtypeStruct((DEPTH, 2, nseq, GLA_HEADS, kw, vw), F32))
    bf_rows = lambda: pltpu.VMEM((seq_len, kw), BF16)
    scratch = [bf_rows() for _ in range(8)] + [
        pltpu.VMEM((seq_len, vw), BF16),
        pltpu.VMEM((seq_len // PAIR, vw, PAIR), BF16),
        pltpu.VMEM((seq_len, vw), F32),
        pltpu.VMEM((seq_len, kw), F32),
        pltpu.VMEM((seq_len, kw), F32),
    ]
    return pl.pallas_call(
        functools.partial(_gla_kernel, seq_len=seq_len, has_s0=has_s0, emit_state=emit_state,
                          alias_state=alias_state),
        grid=(nseq, GLA_HEADS),
        in_specs=in_specs,
        out_specs=out_specs,
        out_shape=out_shape,
        scratch_shapes=scratch,
        input_output_aliases=aliases,
        compiler_params=_cparams(("arbitrary", "arbitrary")),
        name="gla_state" if emit_state else "gla",
    )(*args)


def _window_count(idx, n, w):
    lo = jnp.maximum(idx - w // 2, 0)
    hi = jnp.minimum(idx - w // 2 + w, n)
    return hi - lo


def _pool_project(pooled, gi, w_ref, sc_ref, o_ref, rows):
    cols = slice(gi * POOL_GROUP_DIM, (gi + 1) * POOL_GROUP_DIM)
    y = _dot(pooled.astype(BF16), w_ref[gi]) * sc_ref[:, cols]
    o_ref[rows, cols] = y.astype(BF16)


def _pool1d_kernel(p_ref, w_ref, sc_ref, o_ref, *, seq_len):
    t = lax.broadcasted_iota(jnp.int32, (seq_len, 1), 0)
    for gi, w in enumerate(POOL_WINDOWS):
        x = p_ref[:, gi * POOL_GROUP_DIM:(gi + 1) * POOL_GROUP_DIM]
        acc = jnp.zeros_like(x)
        for j in range(w):
            d = j - w // 2
            valid = jnp.logical_and(t + d >= 0, t + d < seq_len)
            term = x if d == 0 else pltpu.roll(x, (-d) % seq_len, 0)
            acc = acc + jnp.where(valid, term, 0.0)
        cnt = _window_count(t, seq_len, w).astype(F32)
        _pool_project(acc / cnt - x, gi, w_ref, sc_ref, o_ref, slice(None))


def _pool2d_kernel(p_ref, w_ref, sc_ref, o_ref, cs_ref, *, seq_len):
    n_rows = seq_len // GRID_W
    pad = (max(POOL_WINDOWS) // 2) * GRID_W
    n_blk = seq_len // POOL_BLK
    zeros = jnp.zeros((pad, POOL_GROUP_DIM), F32)
    cs_ref[pl.ds(0, pad), :] = zeros
    cs_ref[pl.ds(pad + seq_len, pad), :] = zeros
    local = lax.broadcasted_iota(jnp.int32, (POOL_BLK, 1), 0)
    col = local & (GRID_W - 1)
    for gi, w in enumerate(POOL_WINDOWS):
        cols = slice(gi * POOL_GROUP_DIM, (gi + 1) * POOL_GROUP_DIM)

        def col_sums(i, carry, w=w, cols=cols):
            r = pl.multiple_of(i * POOL_BLK, POOL_BLK)
            x = p_ref[pl.ds(r, POOL_BLK), cols]
            acc = jnp.zeros_like(x)
            for j in range(w):
                d = j - w // 2
                valid = jnp.logical_and(col + d >= 0, col + d < GRID_W)
                term = x if d == 0 else pltpu.roll(x, (-d) % POOL_BLK, 0)
                acc = acc + jnp.where(valid, term, 0.0)
            cs_ref[pl.ds(pl.multiple_of(r + pad, GRID_W), POOL_BLK), :] = acc
            return carry

        lax.fori_loop(0, n_blk, col_sums, 0)

        def row_sums(i, carry, w=w, cols=cols, gi=gi):
            r = pl.multiple_of(i * POOL_BLK, POOL_BLK)
            acc = jnp.zeros((POOL_BLK, POOL_GROUP_DIM), F32)
            for j in range(w):
                off = pad + (j - w // 2) * GRID_W
                acc = acc + cs_ref[pl.ds(pl.multiple_of(r + off, GRID_W), POOL_BLK), :]
            grow = (r + local) >> GRID_SHIFT
            cnt = (_window_count(grow, n_rows, w) * _window_count(col, GRID_W, w)).astype(F32)
            x = p_ref[pl.ds(r, POOL_BLK), cols]
            _pool_project(acc / cnt - x, gi, w_ref, sc_ref, o_ref, pl.ds(r, POOL_BLK))
            return carry

        lax.fori_loop(0, n_blk, row_sums, 0)


def _pool_call(z, w_pool, pool_scale, layer, nseq, seq_len, on_grid):
    m = nseq * seq_len
    if on_grid:
        body = functools.partial(_pool2d_kernel, seq_len=seq_len)
        pad = (max(POOL_WINDOWS) // 2) * GRID_W
        scratch = [pltpu.VMEM((seq_len + 2 * pad, POOL_GROUP_DIM), F32)]
    else:
        body = functools.partial(_pool1d_kernel, seq_len=seq_len)
        scratch = []
    n_groups = len(POOL_WINDOWS)
    return pl.pallas_call(
        body,
        grid=(nseq,),
        in_specs=[
            pl.BlockSpec((seq_len, POOL_WIDTH), lambda s: (s, Z_P // POOL_WIDTH)),
            pl.BlockSpec((None, n_groups, POOL_GROUP_DIM, POOL_GROUP_DIM), lambda s: (layer, 0, 0, 0)),
            _layer_vec_spec(layer, POOL_WIDTH),
        ],
        out_specs=pl.BlockSpec((seq_len, POOL_WIDTH), lambda s: (s, 0)),
        out_shape=jax.ShapeDtypeStruct((m, POOL_WIDTH), BF16),
        scratch_shapes=scratch,
        compiler_params=_cparams(("arbitrary",)),
        name="pool2d" if on_grid else "pool1d",
    )(z, w_pool, pool_scale)


def _mm_out_ln_kernel(mo_ref, mp_ref, w_ref, x_ref, g1_ref, sh2_ref, sc2_ref, lg_ref, lb_ref,
                      x1_ref, h2_ref):
    for s in range(x_ref.shape[0] // LN_SUB):
        rows = pl.ds(s * LN_SUB, LN_SUB)
        m = _dot(mo_ref[rows, :], w_ref[:GLA_WIDTH, :]) + _dot(mp_ref[rows, :], w_ref[GLA_WIDTH:, :])
        x1 = _layer_norm(ALPHA * x_ref[rows, :] + g1_ref[...] * m, lg_ref[...], lb_ref[...])
        x1_ref[rows, :] = x1
        h2_ref[rows, :] = (x1 * (1.0 + sc2_ref[...]) + sh2_ref[...]).astype(BF16)


def _mm_out_ln_call(mo, mp, w_out, x, mod4, rows_per_mod, ln_g, ln_b, layer):
    m = x.shape[0]
    tm = min(512, m)
    row = pl.BlockSpec((tm, D_MODEL), lambda i: (i, 0))
    half = pl.BlockSpec((tm, GLA_WIDTH), lambda i: (i, 0))
    w_spec = pl.BlockSpec((None, D_MODEL, D_MODEL), lambda i: (layer, 0, 0), pipeline_mode=pl.Buffered(1))
    mod = lambda part: _mod_spec(layer, part, rows_per_mod, tm)
    return pl.pallas_call(
        _mm_out_ln_kernel,
        grid=(m // tm,),
        in_specs=[half, half, w_spec, row, mod(2), mod(3), mod(4),
                  _layer_vec_spec(layer, D_MODEL), _layer_vec_spec(layer, D_MODEL)],
        out_specs=[row, row],
        out_shape=[jax.ShapeDtypeStruct((m, D_MODEL), F32), jax.ShapeDtypeStruct((m, D_MODEL), BF16)],
        compiler_params=_cparams(("arbitrary",)),
        name="mm_out_ln",
    )(mo, mp, w_out, x, mod4, mod4, mod4, ln_g, ln_b)


def _mm_gu_kernel(h_ref, wg_ref, wu_ref, o_ref):
    h = h_ref[...]
    gate = _dot(h, wg_ref[...])
    up = _dot(h, wu_ref[...])
    o_ref[...] = (jax.nn.silu(gate) * up).astype(BF16)


def _mm_gu_call(h, w_gu, layer):
    m = h.shape[0]
    tm = min(1024, m)
    tn = 512
    nj = D_FF // tn
    return pl.pallas_call(
        _mm_gu_kernel,
        grid=(nj, m // tm),
        in_specs=[
            pl.BlockSpec((tm, D_MODEL), lambda j, i: (i, 0)),
            pl.BlockSpec((None, D_MODEL, tn), lambda j, i: (layer, 0, j)),
            pl.BlockSpec((None, D_MODEL, tn), lambda j, i: (layer, 0, nj + j)),
        ],
        out_specs=pl.BlockSpec((tm, tn), lambda j, i: (i, j)),
        out_shape=jax.ShapeDtypeStruct((m, D_FF), BF16),
        compiler_params=_cparams(("arbitrary", "arbitrary")),
        name="mm_gu",
    )(h, w_gu, w_gu)


def _mm_down_ln_kernel(*refs, emit_h):
    a_ref, w_ref, x1_ref, g2_ref = refs[:4]
    pos = 4
    if emit_h:
        shn_ref, scn_ref = refs[pos:pos + 2]
        pos += 2
    lg_ref, lb_ref, x_ref = refs[pos:pos + 3]
    pos += 3
    if emit_h:
        h_ref = refs[pos]
    for s in range(x_ref.shape[0] // LN_SUB):
        rows = pl.ds(s * LN_SUB, LN_SUB)
        f = _dot(a_ref[rows, :], w_ref[...])
        x = _layer_norm(ALPHA * x1_ref[rows, :] + g2_ref[...] * f, lg_ref[...], lb_ref[...])
        x_ref[rows, :] = x
        if emit_h:
            h_ref[rows, :] = (x * (1.0 + scn_ref[...]) + shn_ref[...]).astype(BF16)


def _mm_down_ln_call(act, w_down, x1, mod4, rows_per_mod, ln_g, ln_b, layer):
    m = x1.shape[0]
    tm = min(256, m)
    emit_h = layer + 1 < DEPTH
    row = pl.BlockSpec((tm, D_MODEL), lambda i: (i, 0))
    in_specs = [
        pl.BlockSpec((tm, D_FF), lambda i: (i, 0)),
        pl.BlockSpec((None, D_FF, D_MODEL), lambda i: (layer, 0, 0), pipeline_mode=pl.Buffered(1)),
        row,
        _mod_spec(layer, 5, rows_per_mod, tm),
    ]
    args = [act, w_down, x1, mod4]
    if emit_h:
        in_specs += [_mod_spec(layer + 1, 0, rows_per_mod, tm), _mod_spec(layer + 1, 1, rows_per_mod, tm)]
        args += [mod4, mod4]
    in_specs += [_layer_vec_spec(layer, D_MODEL), _layer_vec_spec(layer, D_MODEL)]
    args += [ln_g, ln_b]
    out_specs = [row]
    out_shape = [jax.ShapeDtypeStruct((m, D_MODEL), F32)]
    if emit_h:
        out_specs.append(row)
        out_shape.append(jax.ShapeDtypeStruct((m, D_MODEL), BF16))
    outs = pl.pallas_call(
        functools.partial(_mm_down_ln_kernel, emit_h=emit_h),
        grid=(m // tm,),
        in_specs=in_specs,
        out_specs=out_specs,
        out_shape=out_shape,
        compiler_params=_cparams(("arbitrary",)),
        name="mm_down_ln",
    )(*args)
    return (outs[0], outs[1]) if emit_h else (outs[0], None)


def _prep_weights(w_in, w_a2, b_a, w_pool, w_out, w_gu, w_down):
    depth = w_in.shape[0]
    gate_pad = jnp.zeros((depth, D_MODEL, Z_GATE_PAD - 2 * GLA_GATE_RANK), w_in.dtype)
    w_in_r = jnp.concatenate(
        [w_in[:, :, :W_AF], w_in[:, :, W_P:], w_in[:, :, W_AF:W_P], gate_pad], axis=-1).astype(BF16)
    wgate = jnp.zeros((depth, Z_GATE_PAD, 2 * GLA_KEY_WIDTH), F32)
    wgate = wgate.at[:, :GLA_GATE_RANK, :GLA_KEY_WIDTH].set(w_a2[:, 0])
    wgate = wgate.at[:, GLA_GATE_RANK:2 * GLA_GATE_RANK, GLA_KEY_WIDTH:].set(w_a2[:, 1])
    bgate = b_a.reshape(depth, 1, 2 * GLA_KEY_WIDTH)
    return (w_in_r, wgate.astype(BF16), bgate, w_pool.astype(BF16), w_out.astype(BF16),
            w_gu.astype(BF16), w_down.astype(BF16))


def kernel(x_prompt, x_sample, state_gla, c, c_ctx, ln_in_g, ln_in_b, w_ada, b_ada, w_in, w_a2, b_a,
           gla_norm_g, w_pool, pool_scale, w_out, ln1_g, ln1_b, w_gu, w_down, ln2_g, ln2_b):
    nb_p, len_p, _ = x_prompt.shape
    nb_s, len_s, _ = x_sample.shape
    w_in_r, wgate, bgate, w_pool_b, w_out_b, w_gu_b, w_down_b = _prep_weights(
        w_in, w_a2, b_a, w_pool, w_out, w_gu, w_down)
    gnorm = gla_norm_g.reshape(DEPTH, 1, GLA_WIDTH)
    pscale = pool_scale.reshape(DEPTH, 1, POOL_WIDTH)
    ln1 = (ln1_g.reshape(DEPTH, 1, D_MODEL), ln1_b.reshape(DEPTH, 1, D_MODEL))
    ln2 = (ln2_g.reshape(DEPTH, 1, D_MODEL), ln2_b.reshape(DEPTH, 1, D_MODEL))

    n_mod = -(-(1 + nb_s) // 8) * 8
    c_all = jnp.concatenate(
        [c_ctx[None, :], c, jnp.zeros((n_mod - 1 - nb_s, D_MODEL), c.dtype)], axis=0)
    mod4 = _ada_call(c_all, w_ada, b_ada).reshape(DEPTH, n_mod, 1, 6 * D_MODEL)

    streams = [
        (x_prompt.reshape(nb_p * len_p, D_MODEL), None, nb_p, len_p, False),
        (x_sample.reshape(nb_s * len_s, D_MODEL), len_s, nb_s, len_s, True),
    ]
    results = []
    state_buf = None
    for x_in, rows_per_mod, nseq, seq_len, on_grid in streams:
        x, h = _ln_in_call(x_in, ln_in_g, ln_in_b, mod4, rows_per_mod)
        for l in range(DEPTH):
            z = _mm_in_call(h, w_in_r, l)
            if on_grid:
                (mo,) = _gla_call(z, wgate, bgate, gnorm, l, nseq, seq_len, s0=state_gla)
            else:
                mo, state_buf = _gla_call(z, wgate, bgate, gnorm, l, nseq, seq_len,
                                          emit_state=True, state_buf=state_buf)
            mp = _pool_call(z, w_pool_b, pscale, l, nseq, seq_len, on_grid)
            x1, h2 = _mm_out_ln_call(mo, mp, w_out_b, x, mod4, rows_per_mod, ln1[0], ln1[1], l)
            act = _mm_gu_call(h2, w_gu_b, l)
            x, h = _mm_down_ln_call(act, w_down_b, x1, mod4, rows_per_mod, ln2[0], ln2[1], l)
        results.append(x)
    y_prompt = results[0].reshape(nb_p, len_p, D_MODEL)
    y_sample = results[1].reshape(nb_s, len_s, D_MODEL)
    return (y_prompt, y_sample, state_buf.astype(x_prompt.dtype))
```

```python
import functools

import numpy as np
import jax
import jax.numpy as jnp
from jax import lax
from jax.experimental import pallas as pl
from jax.experimental.pallas import tpu as pltpu

F32 = jnp.float32
BF16 = jnp.bfloat16

D_MODEL = 2048
DEPTH = 4
GLA_HEADS = 4
GLA_DV = 256
GLA_DK = 128
GLA_KEY_WIDTH = GLA_HEADS * GLA_DK
GLA_WIDTH = GLA_HEADS * GLA_DV
POOL_WIDTH = 1024
GLA_GATE_RANK = 16
GLA_GATE_TEMP = 16.0
GLA_CHUNK = 64
POOL_WINDOWS = (2, 4, 8, 16)
POOL_GROUP_DIM = 256
GRID_W = 64
D_FF = 5632
ALPHA = (2 * DEPTH) ** 0.25
LN_EPS = 1e-5
Q_SCALE = GLA_DK ** -0.5

Z_Q = 0
Z_K = Z_Q + GLA_KEY_WIDTH
Z_V = Z_K + GLA_KEY_WIDTH
Z_G = Z_V + GLA_WIDTH
Z_P = Z_G + GLA_WIDTH
Z_GATE = Z_P + POOL_WIDTH
Z_GATE_PAD = 128
Z_WIDTH = Z_GATE + Z_GATE_PAD
W_AF = 2 * GLA_KEY_WIDTH + 2 * GLA_WIDTH
W_P = W_AF + 2 * GLA_GATE_RANK

P1_BLK = 256
POOL_BLK = 256
LN_SUB = 128
CHUNK_SHIFT = GLA_CHUNK.bit_length() - 1
GRID_SHIFT = GRID_W.bit_length() - 1
V7X_VMEM_LIMIT_BYTES = 56 * 2 ** 20


def _log_sigmoid(x):
    return jnp.minimum(x, 0.0) - jnp.log1p(jnp.exp(-jnp.abs(x)))


def _dot(a, b):
    return jnp.dot(a, b, preferred_element_type=F32)


def _dot_nt(a, b):
    return lax.dot_general(a, b, (((1,), (1,)), ((), ())), preferred_element_type=F32)


def _cparams(semantics):
    return pltpu.CompilerParams(dimension_semantics=semantics, vmem_limit_bytes=V7X_VMEM_LIMIT_BYTES)


def _layer_norm(y, g, b):
    mu = jnp.mean(y, axis=-1, keepdims=True)
    yc = y - mu
    var = jnp.mean(yc * yc, axis=-1, keepdims=True)
    return yc * lax.rsqrt(var + LN_EPS) * g + b


def _mod_spec(layer, part, rows_per_mod, tm):
    if rows_per_mod is None:
        return pl.BlockSpec((None, None, 1, D_MODEL), lambda i, *_: (layer, 0, 0, part))
    return pl.BlockSpec((None, None, 1, D_MODEL),
                        lambda i, *_: (layer, 1 + (i * tm) // rows_per_mod, 0, part))


def _layer_vec_spec(layer, width):
    return pl.BlockSpec((None, 1, width), lambda *_: (layer, 0, 0))


def _ada_kernel(c_ref, w_ref, b_ref, o_ref):
    c = c_ref[...]
    s = jax.nn.silu(c).astype(BF16)
    o_ref[...] = _dot(s, w_ref[...].astype(BF16)) + b_ref[...]


def _ada_call(c_all, w_ada, b_ada):
    rows = c_all.shape[0]
    tn = 1024
    nj = (6 * D_MODEL) // tn
    return pl.pallas_call(
        _ada_kernel,
        grid=(DEPTH, nj),
        in_specs=[
            pl.BlockSpec((rows, D_MODEL), lambda l, j: (0, 0)),
            pl.BlockSpec((None, D_MODEL, tn), lambda l, j: (l, 0, j)),
            pl.BlockSpec((None, 1, tn), lambda l, j: (l, 0, j)),
        ],
        out_specs=pl.BlockSpec((None, rows, tn), lambda l, j: (l, 0, j)),
        out_shape=jax.ShapeDtypeStruct((DEPTH, rows, 6 * D_MODEL), F32),
        compiler_params=_cparams(("arbitrary", "arbitrary")),
        name="ada_mod",
    )(c_all, w_ada, b_ada.reshape(DEPTH, 1, 6 * D_MODEL))


def _ln_in_kernel(x_ref, g_ref, b_ref, sh_ref, sc_ref, x0_ref, h_ref):
    y = _layer_norm(x_ref[...], g_ref[...], b_ref[...])
    x0_ref[...] = y
    h_ref[...] = (y * (1.0 + sc_ref[...]) + sh_ref[...]).astype(BF16)


def _ln_in_call(x, g, b, mod4, rows_per_mod):
    m = x.shape[0]
    tm = min(512, m)
    row = pl.BlockSpec((tm, D_MODEL), lambda i: (i, 0))
    vec = pl.BlockSpec((1, D_MODEL), lambda i: (0, 0))
    return pl.pallas_call(
        _ln_in_kernel,
        grid=(m // tm,),
        in_specs=[row, vec, vec, _mod_spec(0, 0, rows_per_mod, tm), _mod_spec(0, 1, rows_per_mod, tm)],
        out_specs=[row, row],
        out_shape=[jax.ShapeDtypeStruct((m, D_MODEL), F32), jax.ShapeDtypeStruct((m, D_MODEL), BF16)],
        compiler_params=_cparams(("arbitrary",)),
        name="ln_in",
    )(x, g.reshape(1, D_MODEL), b.reshape(1, D_MODEL), mod4, mod4)


def _mm_kernel(h_ref, w_ref, o_ref):
    o_ref[...] = _dot(h_ref[...], w_ref[...])


def _mm_in_call(h, w, layer):
    m = h.shape[0]
    tm = min(512, m)
    return pl.pallas_call(
        _mm_kernel,
        grid=(m // tm,),
        in_specs=[
            pl.BlockSpec((tm, D_MODEL), lambda i: (i, 0)),
            pl.BlockSpec((None, D_MODEL, Z_WIDTH), lambda i: (layer, 0, 0), pipeline_mode=pl.Buffered(1)),
        ],
        out_specs=pl.BlockSpec((tm, Z_WIDTH), lambda i: (i, 0)),
        out_shape=jax.ShapeDtypeStruct((m, Z_WIDTH), F32),
        compiler_params=_cparams(("arbitrary",)),
        name="mm_in",
    )(h, w)


def _gla_kernel(*refs, seq_len, has_s0, emit_state, alias_state):
    (q_ref, k_ref, v_ref, g_ref, zg_ref, wg_ref, bg_ref, gn_ref, tl_ref) = refs[:9]
    pos = 9
    if has_s0:
        s0f_ref, s0b_ref = refs[pos:pos + 2]
        pos += 2
    if alias_state:
        pos += 1
    mo_ref = refs[pos]
    pos += 1
    if emit_state:
        st_ref = refs[pos]
        pos += 1
    qef, qeb, o_ref, u_ref, dec_ref = refs[pos:]

    n_blk = seq_len // P1_BLK
    n_chunks = seq_len // GLA_CHUNK
    c = GLA_CHUNK
    dk = GLA_DK

    brow = lax.broadcasted_iota(jnp.int32, (P1_BLK, P1_BLK), 0)
    bcol = lax.broadcasted_iota(jnp.int32, (P1_BLK, P1_BLK), 1)
    same_chunk = (brow >> CHUNK_SHIFT) == (bcol >> CHUNK_SHIFT)
    mask_f = jnp.logical_and(same_chunk, bcol <= brow)
    mask_b = jnp.logical_and(same_chunk, bcol >= brow)
    row_chunk = lax.broadcasted_iota(jnp.int32, (P1_BLK, dk), 0) >> CHUNK_SHIFT

    def phase_a(i, carry):
        r = pl.multiple_of(i * P1_BLK, P1_BLK)
        rows = pl.ds(r, P1_BLK)
        zg = zg_ref[rows, :].astype(BF16)
        la = _log_sigmoid(_dot(zg, wg_ref[...]) + bg_ref[...]) * (1.0 / GLA_GATE_TEMP)
        hi = la.astype(BF16)
        r1 = la - hi.astype(F32)
        mid = r1.astype(BF16)
        lo = (r1 - mid.astype(F32)).astype(BF16)
        tl = tl_ref[...]
        pre = _dot(tl, hi) + _dot(tl, mid) + _dot(tl, lo)
        n_sub = P1_BLK // c
        tot = jnp.concatenate(
            [jnp.broadcast_to(pre[(s + 1) * c - 1:(s + 1) * c, :], (c, 2 * dk)) for s in range(n_sub)], axis=0)
        b_f = pre[:, :dk]
        tot_f = tot[:, :dk]
        tot_b = tot[:, dk:]
        b_b = tot_b - pre[:, dk:] + la[:, dk:]
        q = q_ref[rows, :] * Q_SCALE
        k = k_ref[rows, :]
        qe_f = (q * jnp.exp(b_f)).astype(BF16)
        qe_b = (q * jnp.exp(b_b)).astype(BF16)
        qef[rows, :] = qe_f
        qeb[rows, :] = qe_b
        ke_f = (k * jnp.exp(-b_f)).astype(BF16)
        ke_b = (k * jnp.exp(-b_b)).astype(BF16)
        a = (jnp.where(mask_f, _dot_nt(qe_f, ke_f), 0.0) + jnp.where(mask_b, _dot_nt(qe_b, ke_b), 0.0))
        v = v_ref[rows, :]
        o_ref[rows, :] = _dot(a.astype(BF16), v.astype(BF16))
        kd_f = k * jnp.exp(tot_f - b_f)
        kd_b = k * jnp.exp(tot_b - b_b)
        kd_cols = [jnp.where(row_chunk == s, kd, 0.0).astype(BF16) for kd in (kd_f, kd_b) for s in range(n_sub)]
        u = _dot(v.T.astype(BF16), jnp.concatenate(kd_cols, axis=-1))
        for s in range(n_sub):
            ch = n_sub * i + s
            u_ref[ch] = u[:, s * dk:(s + 1) * dk]
            u_ref[n_chunks + ch] = u[:, (n_sub + s) * dk:(n_sub + s + 1) * dk]
            dec = jnp.exp(tot[s * c:s * c + 8, :])
            dec_ref[pl.ds(pl.multiple_of(8 * ch, 8), 8), :] = dec[:, :dk]
            dec_ref[pl.ds(pl.multiple_of(8 * (n_chunks + ch), 8), 8), :] = dec[:, dk:]
        return carry

    lax.fori_loop(0, n_blk, phase_a, 0, unroll=2 if n_blk % 2 == 0 else 1)

    def fwd_chunk(ch, st):
        rows = pl.ds(pl.multiple_of(ch * c, c), c)
        o_ref[rows, :] += _dot_nt(qef[rows, :], st.astype(BF16))
        return st * dec_ref[pl.ds(pl.multiple_of(8 * ch, 8), 1), :] + u_ref[ch]

    def bwd_chunk(j, st):
        ch = n_chunks - 1 - j
        rows = pl.ds(pl.multiple_of(ch * c, c), c)
        o = o_ref[rows, :] + _dot_nt(qeb[rows, :], st.astype(BF16))
        ms = jnp.mean(o * o, axis=-1, keepdims=True)
        y = o * lax.rsqrt(ms + LN_EPS) * gn_ref[...]
        mo_ref[rows, :] = (y * jax.nn.silu(g_ref[rows, :])).astype(BF16)
        return st * dec_ref[pl.ds(pl.multiple_of(8 * (n_chunks + ch), 8), 1), :] + u_ref[n_chunks + ch]

    if has_s0:
        st0_f = s0f_ref[...].T
        st0_b = s0b_ref[...].T
    else:
        st0_f = jnp.zeros((GLA_DV, dk), F32)
        st0_b = st0_f
    unroll = True if n_chunks <= 4 else 4
    st_f = lax.fori_loop(0, n_chunks, fwd_chunk, st0_f, unroll=unroll)
    st_b = lax.fori_loop(0, n_chunks, bwd_chunk, st0_b, unroll=unroll)
    if emit_state:
        st_ref[0] = st_f.T
        st_ref[1] = st_b.T


def _chunk_prefix_matrix():
    i = np.arange(P1_BLK)
    same = (i[:, None] >> CHUNK_SHIFT) == (i[None, :] >> CHUNK_SHIFT)
    return jnp.asarray(np.logical_and(same, i[None, :] <= i[:, None]).astype(np.float32), BF16)


def _gla_call(z, wgate, bgate, gnorm, layer, nseq, seq_len, s0=None, emit_state=False, state_buf=None):
    m = nseq * seq_len
    tl = _chunk_prefix_matrix()
    has_s0 = s0 is not None
    alias_state = state_buf is not None
    kw, vw = GLA_DK, GLA_DV
    in_specs = [
        pl.BlockSpec((seq_len, kw), lambda s, h: (s, Z_Q // kw + h)),
        pl.BlockSpec((seq_len, kw), lambda s, h: (s, Z_K // kw + h)),
        pl.BlockSpec((seq_len, vw), lambda s, h: (s, Z_V // vw + h)),
        pl.BlockSpec((seq_len, vw), lambda s, h: (s, Z_G // vw + h)),
        pl.BlockSpec((seq_len, Z_GATE_PAD), lambda s, h: (s, Z_GATE // Z_GATE_PAD)),
        pl.BlockSpec((None, Z_GATE_PAD, 2 * kw), lambda s, h: (layer, 0, h)),
        pl.BlockSpec((None, 1, 2 * kw), lambda s, h: (layer, 0, h)),
        pl.BlockSpec((None, 1, vw), lambda s, h: (layer, 0, h)),
        pl.BlockSpec((P1_BLK, P1_BLK), lambda s, h: (0, 0)),
    ]
    args = [z, z, z, z, z, wgate, bgate, gnorm, tl]
    if has_s0:
        in_specs += [
            pl.BlockSpec((None, None, None, None, kw, vw), lambda s, h: (layer, 0, s, h, 0, 0)),
            pl.BlockSpec((None, None, None, None, kw, vw), lambda s, h: (layer, 1, s, h, 0, 0)),
        ]
        args += [s0, s0]
    aliases = {}
    if alias_state:
        aliases = {len(args): 1}
        in_specs.append(pl.BlockSpec(memory_space=pl.ANY))
        args.append(state_buf)
    out_specs = [pl.BlockSpec((seq_len, vw), lambda s, h: (s, h))]
    out_shape = [jax.ShapeDtypeStruct((m, GLA_WIDTH), BF16)]
    if emit_state:
        out_specs.append(pl.BlockSpec((None, 2, None, None, kw, vw), lambda s, h: (layer, 0, s, h, 0, 0)))
        out_shape.append(jax.ShapeDtypeStruct((DEPTH, 2, nseq, GLA_HEADS, kw, vw), F32))
    n_chunks = seq_len // GLA_CHUNK
    scratch = [
        pltpu.VMEM((seq_len, kw), BF16),
        pltpu.VMEM((seq_len, kw), BF16),
        pltpu.VMEM((seq_len, vw), F32),
        pltpu.VMEM((2 * n_chunks, vw, kw), F32),
        pltpu.VMEM((2 * n_chunks * 8, kw), F32),
    ]
    return pl.pallas_call(
        functools.partial(_gla_kernel, seq_len=seq_len, has_s0=has_s0, emit_state=emit_state,
                          alias_state=alias_state),
        grid=(nseq, GLA_HEADS),
        in_specs=in_specs,
        out_specs=out_specs,
        out_shape=out_shape,
        scratch_shapes=scratch,
        input_output_aliases=aliases,
        compiler_params=_cparams(("arbitrary", "arbitrary")),
        name="gla_state" if emit_state else "gla",
    )(*args)


def _window_count(idx, n, w):
    lo = jnp.maximum(idx - w // 2, 0)
    hi = jnp.minimum(idx - w // 2 + w, n)
    return hi - lo


def _pool_project(pooled, gi, w_ref, sc_ref, o_ref, rows):
    cols = slice(gi * POOL_GROUP_DIM, (gi + 1) * POOL_GROUP_DIM)
    y = _dot(pooled.astype(BF16), w_ref[gi]) * sc_ref[:, cols]
    o_ref[rows, cols] = y.astype(BF16)


def _pool1d_kernel(p_ref, w_ref, sc_ref, o_ref, *, seq_len):
    t = lax.broadcasted_iota(jnp.int32, (seq_len, 1), 0)
    for gi, w in enumerate(POOL_WINDOWS):
        x = p_ref[:, gi * POOL_GROUP_DIM:(gi + 1) * POOL_GROUP_DIM]
        acc = jnp.zeros_like(x)
        for j in range(w):
            d = j - w // 2
            valid = jnp.logical_and(t + d >= 0, t + d < seq_len)
            term = x if d == 0 else pltpu.roll(x, (-d) % seq_len, 0)
            acc = acc + jnp.where(valid, term, 0.0)
        cnt = _window_count(t, seq_len, w).astype(F32)
        _pool_project(acc / cnt - x, gi, w_ref, sc_ref, o_ref, slice(None))


def _pool2d_kernel(p_ref, w_ref, sc_ref, o_ref, cs_ref, *, seq_len):
    n_rows = seq_len // GRID_W
    pad = (max(POOL_WINDOWS) // 2) * GRID_W
    n_blk = seq_len // POOL_BLK
    zeros = jnp.zeros((pad, POOL_GROUP_DIM), F32)
    cs_ref[pl.ds(0, pad), :] = zeros
    cs_ref[pl.ds(pad + seq_len, pad), :] = zeros
    local = lax.broadcasted_iota(jnp.int32, (POOL_BLK, 1), 0)
    col = local & (GRID_W - 1)
    for gi, w in enumerate(POOL_WINDOWS):
        cols = slice(gi * POOL_GROUP_DIM, (gi + 1) * POOL_GROUP_DIM)

        def col_sums(i, carry, w=w, cols=cols):
            r = pl.multiple_of(i * POOL_BLK, POOL_BLK)
            x = p_ref[pl.ds(r, POOL_BLK), cols]
            acc = jnp.zeros_like(x)
            for j in range(w):
                d = j - w // 2
                valid = jnp.logical_and(col + d >= 0, col + d < GRID_W)
                term = x if d == 0 else pltpu.roll(x, (-d) % POOL_BLK, 0)
                acc = acc + jnp.where(valid, term, 0.0)
            cs_ref[pl.ds(pl.multiple_of(r + pad, GRID_W), POOL_BLK), :] = acc
            return carry

        lax.fori_loop(0, n_blk, col_sums, 0)

        def row_sums(i, carry, w=w, cols=cols, gi=gi):
            r = pl.multiple_of(i * POOL_BLK, POOL_BLK)
            acc = jnp.zeros((POOL_BLK, POOL_GROUP_DIM), F32)
            for j in range(w):
                off = pad + (j - w // 2) * GRID_W
                acc = acc + cs_ref[pl.ds(pl.multiple_of(r + off, GRID_W), POOL_BLK), :]
            grow = (r + local) >> GRID_SHIFT
            cnt = (_window_count(grow, n_rows, w) * _window_count(col, GRID_W, w)).astype(F32)
            x = p_ref[pl.ds(r, POOL_BLK), cols]
            _pool_project(acc / cnt - x, gi, w_ref, sc_ref, o_ref, pl.ds(r, POOL_BLK))
            return carry

        lax.fori_loop(0, n_blk, row_sums, 0)


def _pool_call(z, w_pool, pool_scale, layer, nseq, seq_len, on_grid):
    m = nseq * seq_len
    if on_grid:
        body = functools.partial(_pool2d_kernel, seq_len=seq_len)
        pad = (max(POOL_WINDOWS) // 2) * GRID_W
        scratch = [pltpu.VMEM((seq_len + 2 * pad, POOL_GROUP_DIM), F32)]
    else:
        body = functools.partial(_pool1d_kernel, seq_len=seq_len)
        scratch = []
    n_groups = len(POOL_WINDOWS)
    return pl.pallas_call(
        body,
        grid=(nseq,),
        in_specs=[
            pl.BlockSpec((seq_len, POOL_WIDTH), lambda s: (s, Z_P // POOL_WIDTH)),
            pl.BlockSpec((None, n_groups, POOL_GROUP_DIM, POOL_GROUP_DIM), lambda s: (layer, 0, 0, 0)),
            _layer_vec_spec(layer, POOL_WIDTH),
        ],
        out_specs=pl.BlockSpec((seq_len, POOL_WIDTH), lambda s: (s, 0)),
        out_shape=jax.ShapeDtypeStruct((m, POOL_WIDTH), BF16),
        scratch_shapes=scratch,
        compiler_params=_cparams(("arbitrary",)),
        name="pool2d" if on_grid else "pool1d",
    )(z, w_pool, pool_scale)


def _mm_out_ln_kernel(mo_ref, mp_ref, w_ref, x_ref, g1_ref, sh2_ref, sc2_ref, lg_ref, lb_ref,
                      x1_ref, h2_ref):
    for s in range(x_ref.shape[0] // LN_SUB):
        rows = pl.ds(s * LN_SUB, LN_SUB)
        m = _dot(mo_ref[rows, :], w_ref[:GLA_WIDTH, :]) + _dot(mp_ref[rows, :], w_ref[GLA_WIDTH:, :])
        x1 = _layer_norm(ALPHA * x_ref[rows, :] + g1_ref[...] * m, lg_ref[...], lb_ref[...])
        x1_ref[rows, :] = x1
        h2_ref[rows, :] = (x1 * (1.0 + sc2_ref[...]) + sh2_ref[...]).astype(BF16)


def _mm_out_ln_call(mo, mp, w_out, x, mod4, rows_per_mod, ln_g, ln_b, layer):
    m = x.shape[0]
    tm = min(512, m)
    row = pl.BlockSpec((tm, D_MODEL), lambda i: (i, 0))
    half = pl.BlockSpec((tm, GLA_WIDTH), lambda i: (i, 0))
    w_spec = pl.BlockSpec((None, D_MODEL, D_MODEL), lambda i: (layer, 0, 0), pipeline_mode=pl.Buffered(1))
    mod = lambda part: _mod_spec(layer, part, rows_per_mod, tm)
    return pl.pallas_call(
        _mm_out_ln_kernel,
        grid=(m // tm,),
        in_specs=[half, half, w_spec, row, mod(2), mod(3), mod(4),
                  _layer_vec_spec(layer, D_MODEL), _layer_vec_spec(layer, D_MODEL)],
        out_specs=[row, row],
        out_shape=[jax.ShapeDtypeStruct((m, D_MODEL), F32), jax.ShapeDtypeStruct((m, D_MODEL), BF16)],
        compiler_params=_cparams(("arbitrary",)),
        name="mm_out_ln",
    )(mo, mp, w_out, x, mod4, mod4, mod4, ln_g, ln_b)


def _mm_gu_kernel(h_ref, wg_ref, wu_ref, o_ref):
    h = h_ref[...]
    gate = _dot(h, wg_ref[...])
    up = _dot(h, wu_ref[...])
    o_ref[...] = (jax.nn.silu(gate) * up).astype(BF16)


def _mm_gu_call(h, w_gu, layer):
    m = h.shape[0]
    tm = min(1024, m)
    tn = 512
    nj = D_FF // tn
    return pl.pallas_call(
        _mm_gu_kernel,
        grid=(nj, m // tm),
        in_specs=[
            pl.BlockSpec((tm, D_MODEL), lambda j, i: (i, 0)),
            pl.BlockSpec((None, D_MODEL, tn), lambda j, i: (layer, 0, j)),
            pl.BlockSpec((None, D_MODEL, tn), lambda j, i: (layer, 0, nj + j)),
        ],
        out_specs=pl.BlockSpec((tm, tn), lambda j, i: (i, j)),
        out_shape=jax.ShapeDtypeStruct((m, D_FF), BF16),
        compiler_params=_cparams(("arbitrary", "arbitrary")),
        name="mm_gu",
    )(h, w_gu, w_gu)


def _mm_down_ln_kernel(*refs, emit_h):
    a_ref, w_ref, x1_ref, g2_ref = refs[:4]
    pos = 4
    if emit_h:
        shn_ref, scn_ref = refs[pos:pos + 2]
        pos += 2
    lg_ref, lb_ref, x_ref = refs[pos:pos + 3]
    pos += 3
    if emit_h:
        h_ref = refs[pos]
    for s in range(x_ref.shape[0] // LN_SUB):
        rows = pl.ds(s * LN_SUB, LN_SUB)
        f = _dot(a_ref[rows, :], w_ref[...])
        x = _layer_norm(ALPHA * x1_ref[rows, :] + g2_ref[...] * f, lg_ref[...], lb_ref[...])
        x_ref[rows, :] = x
        if emit_h:
            h_ref[rows, :] = (x * (1.0 + scn_ref[...]) + shn_ref[...]).astype(BF16)


def _mm_down_ln_call(act, w_down, x1, mod4, rows_per_mod, ln_g, ln_b, layer):
    m = x1.shape[0]
    tm = min(256, m)
    emit_h = layer + 1 < DEPTH
    row = pl.BlockSpec((tm, D_MODEL), lambda i: (i, 0))
    in_specs = [
        pl.BlockSpec((tm, D_FF), lambda i: (i, 0)),
        pl.BlockSpec((None, D_FF, D_MODEL), lambda i: (layer, 0, 0), pipeline_mode=pl.Buffered(1)),
        row,
        _mod_spec(layer, 5, rows_per_mod, tm),
    ]
    args = [act, w_down, x1, mod4]
    if emit_h:
        in_specs += [_mod_spec(layer + 1, 0, rows_per_mod, tm), _mod_spec(layer + 1, 1, rows_per_mod, tm)]
        args += [mod4, mod4]
    in_specs += [_layer_vec_spec(layer, D_MODEL), _layer_vec_spec(layer, D_MODEL)]
    args += [ln_g, ln_b]
    out_specs = [row]
    out_shape = [jax.ShapeDtypeStruct((m, D_MODEL), F32)]
    if emit_h:
        out_specs.append(row)
        out_shape.append(jax.ShapeDtypeStruct((m, D_MODEL), BF16))
    outs = pl.pallas_call(
        functools.partial(_mm_down_ln_kernel, emit_h=emit_h),
        grid=(m // tm,),
        in_specs=in_specs,
        out_specs=out_specs,
        out_shape=out_shape,
        compiler_params=_cparams(("arbitrary",)),
        name="mm_down_ln",
    )(*args)
    return (outs[0], outs[1]) if emit_h else (outs[0], None)


def _prep_weights(w_in, w_a2, b_a, w_pool, w_out, w_gu, w_down):
    depth = w_in.shape[0]
    gate_pad = jnp.zeros((depth, D_MODEL, Z_GATE_PAD - 2 * GLA_GATE_RANK), w_in.dtype)
    w_in_r = jnp.concatenate(
        [w_in[:, :, :W_AF], w_in[:, :, W_P:], w_in[:, :, W_AF:W_P], gate_pad], axis=-1).astype(BF16)
    wgate = jnp.zeros((depth, Z_GATE_PAD, GLA_HEADS, 2, GLA_DK), F32)
    per_head = lambda t: t.reshape(depth, GLA_GATE_RANK, GLA_HEADS, GLA_DK)
    wgate = wgate.at[:, :GLA_GATE_RANK, :, 0, :].set(per_head(w_a2[:, 0]))
    wgate = wgate.at[:, GLA_GATE_RANK:2 * GLA_GATE_RANK, :, 1, :].set(per_head(w_a2[:, 1]))
    wgate = wgate.reshape(depth, Z_GATE_PAD, 2 * GLA_KEY_WIDTH)
    bgate = b_a.reshape(depth, 2, GLA_HEADS, GLA_DK).transpose(0, 2, 1, 3).reshape(depth, 1, 2 * GLA_KEY_WIDTH)
    return (w_in_r, wgate.astype(BF16), bgate, w_pool.astype(BF16), w_out.astype(BF16),
            w_gu.astype(BF16), w_down.astype(BF16))


def kernel(x_prompt, x_sample, state_gla, c, c_ctx, ln_in_g, ln_in_b, w_ada, b_ada, w_in, w_a2, b_a,
           gla_norm_g, w_pool, pool_scale, w_out, ln1_g, ln1_b, w_gu, w_down, ln2_g, ln2_b):
    nb_p, len_p, _ = x_prompt.shape
    nb_s, len_s, _ = x_sample.shape
    w_in_r, wgate, bgate, w_pool_b, w_out_b, w_gu_b, w_down_b = _prep_weights(
        w_in, w_a2, b_a, w_pool, w_out, w_gu, w_down)
    gnorm = gla_norm_g.reshape(DEPTH, 1, GLA_WIDTH)
    pscale = pool_scale.reshape(DEPTH, 1, POOL_WIDTH)
    ln1 = (ln1_g.reshape(DEPTH, 1, D_MODEL), ln1_b.reshape(DEPTH, 1, D_MODEL))
    ln2 = (ln2_g.reshape(DEPTH, 1, D_MODEL), ln2_b.reshape(DEPTH, 1, D_MODEL))

    n_mod = -(-(1 + nb_s) // 8) * 8
    c_all = jnp.concatenate(
        [c_ctx[None, :], c, jnp.zeros((n_mod - 1 - nb_s, D_MODEL), c.dtype)], axis=0)
    mod4 = _ada_call(c_all, w_ada, b_ada).reshape(DEPTH, n_mod, 1, 6 * D_MODEL)

    streams = [
        (x_prompt.reshape(nb_p * len_p, D_MODEL), None, nb_p, len_p, False),
        (x_sample.reshape(nb_s * len_s, D_MODEL), len_s, nb_s, len_s, True),
    ]
    results = []
    state_buf = None
    for x_in, rows_per_mod, nseq, seq_len, on_grid in streams:
        x, h = _ln_in_call(x_in, ln_in_g, ln_in_b, mod4, rows_per_mod)
        for l in range(DEPTH):
            z = _mm_in_call(h, w_in_r, l)
            if on_grid:
                (mo,) = _gla_call(z, wgate, bgate, gnorm, l, nseq, seq_len, s0=state_gla)
            else:
                mo, state_buf = _gla_call(z, wgate, bgate, gnorm, l, nseq, seq_len,
                                          emit_state=True, state_buf=state_buf)
            mp = _pool_call(z, w_pool_b, pscale, l, nseq, seq_len, on_grid)
            x1, h2 = _mm_out_ln_call(mo, mp, w_out_b, x, mod4, rows_per_mod, ln1[0], ln1[1], l)
            act = _mm_gu_call(h2, w_gu_b, l)
            x, h = _mm_down_ln_call(act, w_down_b, x1, mod4, rows_per_mod, ln2[0], ln2[1], l)
        results.append(x)
    y_prompt = results[0].reshape(nb_p, len_p, D_MODEL)
    y_sample = results[1].reshape(nb_s, len_s, D_MODEL)
    return (y_prompt, y_sample, state_buf.astype(x_prompt.dtype))
```

```python
import functools

import numpy as np
import jax
import jax.numpy as jnp
from jax import lax
from jax.experimental import pallas as pl
from jax.experimental.pallas import tpu as pltpu

F32 = jnp.float32
BF16 = jnp.bfloat16

D_MODEL = 2048
DEPTH = 4
GLA_HEADS = 4
GLA_DV = 256
GLA_DK = 128
GLA_KEY_WIDTH = GLA_HEADS * GLA_DK
GLA_WIDTH = GLA_HEADS * GLA_DV
POOL_WIDTH = 1024
GLA_GATE_RANK = 16
GLA_GATE_TEMP = 16.0
GLA_CHUNK = 64
POOL_WINDOWS = (2, 4, 8, 16)
POOL_GROUP_DIM = 256
GRID_W = 64
D_FF = 5632
ALPHA = (2 * DEPTH) ** 0.25
LN_EPS = 1e-5
Q_SCALE = GLA_DK ** -0.5

Z_Q = 0
Z_K = Z_Q + GLA_KEY_WIDTH
Z_V = Z_K + GLA_KEY_WIDTH
Z_G = Z_V + GLA_WIDTH
Z_P = Z_G + GLA_WIDTH
Z_GATE = Z_P + POOL_WIDTH
Z_GATE_PAD = 128
Z_WIDTH = Z_GATE + Z_GATE_PAD
W_AF = 2 * GLA_KEY_WIDTH + 2 * GLA_WIDTH
W_P = W_AF + 2 * GLA_GATE_RANK

P1_BLK = 256
POOL_BLK = 256
LN_SUB = 128
CHUNK_SHIFT = GLA_CHUNK.bit_length() - 1
GRID_SHIFT = GRID_W.bit_length() - 1
V7X_VMEM_LIMIT_BYTES = 56 * 2 ** 20


def _log_sigmoid(x):
    return jnp.minimum(x, 0.0) - jnp.log1p(jnp.exp(-jnp.abs(x)))


def _dot(a, b):
    return jnp.dot(a, b, preferred_element_type=F32)


def _dot_nt(a, b):
    return lax.dot_general(a, b, (((1,), (1,)), ((), ())), preferred_element_type=F32)


def _band_sum(band, x):
    hi = x.astype(BF16)
    r1 = x - hi.astype(F32)
    mid = r1.astype(BF16)
    lo = (r1 - mid.astype(F32)).astype(BF16)
    return _dot(band, hi) + _dot(band, mid) + _dot(band, lo)


def _cparams(semantics):
    return pltpu.CompilerParams(dimension_semantics=semantics, vmem_limit_bytes=V7X_VMEM_LIMIT_BYTES)


def _layer_norm(y, g, b):
    mu = jnp.mean(y, axis=-1, keepdims=True)
    yc = y - mu
    var = jnp.mean(yc * yc, axis=-1, keepdims=True)
    return yc * lax.rsqrt(var + LN_EPS) * g + b


def _mod_spec(layer, part, rows_per_mod, tm):
    if rows_per_mod is None:
        return pl.BlockSpec((None, None, 1, D_MODEL), lambda i, *_: (layer, 0, 0, part))
    return pl.BlockSpec((None, None, 1, D_MODEL),
                        lambda i, *_: (layer, 1 + (i * tm) // rows_per_mod, 0, part))


def _layer_vec_spec(layer, width):
    return pl.BlockSpec((None, 1, width), lambda *_: (layer, 0, 0))


def _ada_kernel(c_ref, w_ref, b_ref, o_ref):
    c = c_ref[...]
    s = jax.nn.silu(c).astype(BF16)
    o_ref[...] = _dot(s, w_ref[...].astype(BF16)) + b_ref[...]


def _ada_call(c_all, w_ada, b_ada):
    rows = c_all.shape[0]
    tn = 1024
    nj = (6 * D_MODEL) // tn
    return pl.pallas_call(
        _ada_kernel,
        grid=(DEPTH, nj),
        in_specs=[
            pl.BlockSpec((rows, D_MODEL), lambda l, j: (0, 0)),
            pl.BlockSpec((None, D_MODEL, tn), lambda l, j: (l, 0, j)),
            pl.BlockSpec((None, 1, tn), lambda l, j: (l, 0, j)),
        ],
        out_specs=pl.BlockSpec((None, rows, tn), lambda l, j: (l, 0, j)),
        out_shape=jax.ShapeDtypeStruct((DEPTH, rows, 6 * D_MODEL), F32),
        compiler_params=_cparams(("arbitrary", "arbitrary")),
        name="ada_mod",
    )(c_all, w_ada, b_ada.reshape(DEPTH, 1, 6 * D_MODEL))


def _ln_in_kernel(x_ref, g_ref, b_ref, sh_ref, sc_ref, x0_ref, h_ref):
    y = _layer_norm(x_ref[...], g_ref[...], b_ref[...])
    x0_ref[...] = y
    h_ref[...] = (y * (1.0 + sc_ref[...]) + sh_ref[...]).astype(BF16)


def _ln_in_call(x, g, b, mod4, rows_per_mod):
    m = x.shape[0]
    tm = min(512, m)
    row = pl.BlockSpec((tm, D_MODEL), lambda i: (i, 0))
    vec = pl.BlockSpec((1, D_MODEL), lambda i: (0, 0))
    return pl.pallas_call(
        _ln_in_kernel,
        grid=(m // tm,),
        in_specs=[row, vec, vec, _mod_spec(0, 0, rows_per_mod, tm), _mod_spec(0, 1, rows_per_mod, tm)],
        out_specs=[row, row],
        out_shape=[jax.ShapeDtypeStruct((m, D_MODEL), F32), jax.ShapeDtypeStruct((m, D_MODEL), BF16)],
        compiler_params=_cparams(("arbitrary",)),
        name="ln_in",
    )(x, g.reshape(1, D_MODEL), b.reshape(1, D_MODEL), mod4, mod4)


def _mm_kernel(h_ref, w_ref, o_ref):
    o_ref[...] = _dot(h_ref[...], w_ref[...])


def _mm_in_call(h, w, layer):
    m = h.shape[0]
    tm = min(512, m)
    return pl.pallas_call(
        _mm_kernel,
        grid=(m // tm,),
        in_specs=[
            pl.BlockSpec((tm, D_MODEL), lambda i: (i, 0)),
            pl.BlockSpec((None, D_MODEL, Z_WIDTH), lambda i: (layer, 0, 0), pipeline_mode=pl.Buffered(1)),
        ],
        out_specs=pl.BlockSpec((tm, Z_WIDTH), lambda i: (i, 0)),
        out_shape=jax.ShapeDtypeStruct((m, Z_WIDTH), F32),
        compiler_params=_cparams(("arbitrary",)),
        name="mm_in",
    )(h, w)


def _gla_kernel(*refs, seq_len, has_s0, emit_state, alias_state):
    (q_ref, k_ref, v_ref, g_ref, zg_ref, wg_ref, bg_ref, gn_ref, tl_ref) = refs[:9]
    pos = 9
    if has_s0:
        s0f_ref, s0b_ref = refs[pos:pos + 2]
        pos += 2
    if alias_state:
        pos += 1
    mo_ref = refs[pos]
    pos += 1
    if emit_state:
        st_ref = refs[pos]
        pos += 1
    qef, qeb, o_ref, u_ref, dec_ref = refs[pos:]

    n_blk = seq_len // P1_BLK
    n_chunks = seq_len // GLA_CHUNK
    c = GLA_CHUNK
    dk = GLA_DK

    brow = lax.broadcasted_iota(jnp.int32, (P1_BLK, P1_BLK), 0)
    bcol = lax.broadcasted_iota(jnp.int32, (P1_BLK, P1_BLK), 1)
    same_chunk = (brow >> CHUNK_SHIFT) == (bcol >> CHUNK_SHIFT)
    mask_f = jnp.logical_and(same_chunk, bcol <= brow)
    mask_b = jnp.logical_and(same_chunk, bcol >= brow)
    row_chunk = lax.broadcasted_iota(jnp.int32, (P1_BLK, dk), 0) >> CHUNK_SHIFT

    def phase_a(i, carry):
        r = pl.multiple_of(i * P1_BLK, P1_BLK)
        rows = pl.ds(r, P1_BLK)
        zg = zg_ref[rows, :].astype(BF16)
        la = _log_sigmoid(_dot(zg, wg_ref[...]) + bg_ref[...]) * (1.0 / GLA_GATE_TEMP)
        pre = _band_sum(tl_ref[...], la)
        n_sub = P1_BLK // c
        tot = jnp.concatenate(
            [jnp.broadcast_to(pre[(s + 1) * c - 1:(s + 1) * c, :], (c, 2 * dk)) for s in range(n_sub)], axis=0)
        b_f = pre[:, :dk]
        tot_f = tot[:, :dk]
        tot_b = tot[:, dk:]
        b_b = tot_b - pre[:, dk:] + la[:, dk:]
        q = q_ref[rows, :] * Q_SCALE
        k = k_ref[rows, :]
        qe_f = (q * jnp.exp(b_f)).astype(BF16)
        qe_b = (q * jnp.exp(b_b)).astype(BF16)
        qef[rows, :] = qe_f
        qeb[rows, :] = qe_b
        ke_f = (k * jnp.exp(-b_f)).astype(BF16)
        ke_b = (k * jnp.exp(-b_b)).astype(BF16)
        a = (jnp.where(mask_f, _dot_nt(qe_f, ke_f), 0.0) + jnp.where(mask_b, _dot_nt(qe_b, ke_b), 0.0))
        v = v_ref[rows, :]
        o_ref[rows, :] = _dot(a.astype(BF16), v.astype(BF16))
        kd_f = k * jnp.exp(tot_f - b_f)
        kd_b = k * jnp.exp(tot_b - b_b)
        kd_cols = [jnp.where(row_chunk == s, kd, 0.0).astype(BF16) for kd in (kd_f, kd_b) for s in range(n_sub)]
        u = _dot(v.T.astype(BF16), jnp.concatenate(kd_cols, axis=-1))
        for s in range(n_sub):
            ch = n_sub * i + s
            u_ref[ch] = u[:, s * dk:(s + 1) * dk]
            u_ref[n_chunks + ch] = u[:, (n_sub + s) * dk:(n_sub + s + 1) * dk]
            dec = jnp.exp(tot[s * c:s * c + 8, :])
            dec_ref[pl.ds(pl.multiple_of(8 * ch, 8), 8), :] = dec[:, :dk]
            dec_ref[pl.ds(pl.multiple_of(8 * (n_chunks + ch), 8), 8), :] = dec[:, dk:]
        return carry

    lax.fori_loop(0, n_blk, phase_a, 0, unroll=2 if n_blk % 2 == 0 else 1)

    def fwd_chunk(ch, st):
        rows = pl.ds(pl.multiple_of(ch * c, c), c)
        o_ref[rows, :] += _dot_nt(qef[rows, :], st.astype(BF16))
        return st * dec_ref[pl.ds(pl.multiple_of(8 * ch, 8), 1), :] + u_ref[ch]

    def bwd_chunk(j, st):
        ch = n_chunks - 1 - j
        rows = pl.ds(pl.multiple_of(ch * c, c), c)
        o = o_ref[rows, :] + _dot_nt(qeb[rows, :], st.astype(BF16))
        ms = jnp.mean(o * o, axis=-1, keepdims=True)
        y = o * lax.rsqrt(ms + LN_EPS) * gn_ref[...]
        mo_ref[rows, :] = (y * jax.nn.silu(g_ref[rows, :])).astype(BF16)
        return st * dec_ref[pl.ds(pl.multiple_of(8 * (n_chunks + ch), 8), 1), :] + u_ref[n_chunks + ch]

    if has_s0:
        st0_f = s0f_ref[...].T
        st0_b = s0b_ref[...].T
    else:
        st0_f = jnp.zeros((GLA_DV, dk), F32)
        st0_b = st0_f
    unroll = True if n_chunks <= 4 else 4
    st_f = lax.fori_loop(0, n_chunks, fwd_chunk, st0_f, unroll=unroll)
    st_b = lax.fori_loop(0, n_chunks, bwd_chunk, st0_b, unroll=unroll)
    if emit_state:
        st_ref[0] = st_f.T
        st_ref[1] = st_b.T


def _chunk_prefix_matrix():
    i = np.arange(P1_BLK)
    same = (i[:, None] >> CHUNK_SHIFT) == (i[None, :] >> CHUNK_SHIFT)
    return jnp.asarray(np.logical_and(same, i[None, :] <= i[:, None]).astype(np.float32), BF16)


def _gla_call(z, wgate, bgate, gnorm, layer, nseq, seq_len, s0=None, emit_state=False, state_buf=None):
    m = nseq * seq_len
    tl = _chunk_prefix_matrix()
    has_s0 = s0 is not None
    alias_state = state_buf is not None
    kw, vw = GLA_DK, GLA_DV
    in_specs = [
        pl.BlockSpec((seq_len, kw), lambda s, h: (s, Z_Q // kw + h)),
        pl.BlockSpec((seq_len, kw), lambda s, h: (s, Z_K // kw + h)),
        pl.BlockSpec((seq_len, vw), lambda s, h: (s, Z_V // vw + h)),
        pl.BlockSpec((seq_len, vw), lambda s, h: (s, Z_G // vw + h)),
        pl.BlockSpec((seq_len, Z_GATE_PAD), lambda s, h: (s, Z_GATE // Z_GATE_PAD)),
        pl.BlockSpec((None, Z_GATE_PAD, 2 * kw), lambda s, h: (layer, 0, h)),
        pl.BlockSpec((None, 1, 2 * kw), lambda s, h: (layer, 0, h)),
        pl.BlockSpec((None, 1, vw), lambda s, h: (layer, 0, h)),
        pl.BlockSpec((P1_BLK, P1_BLK), lambda s, h: (0, 0)),
    ]
    args = [z, z, z, z, z, wgate, bgate, gnorm, tl]
    if has_s0:
        in_specs += [
            pl.BlockSpec((None, None, None, None, kw, vw), lambda s, h: (layer, 0, s, h, 0, 0)),
            pl.BlockSpec((None, None, None, None, kw, vw), lambda s, h: (layer, 1, s, h, 0, 0)),
        ]
        args += [s0, s0]
    aliases = {}
    if alias_state:
        aliases = {len(args): 1}
        in_specs.append(pl.BlockSpec(memory_space=pl.ANY))
        args.append(state_buf)
    out_specs = [pl.BlockSpec((seq_len, vw), lambda s, h: (s, h))]
    out_shape = [jax.ShapeDtypeStruct((m, GLA_WIDTH), BF16)]
    if emit_state:
        out_specs.append(pl.BlockSpec((None, 2, None, None, kw, vw), lambda s, h: (layer, 0, s, h, 0, 0)))
        out_shape.append(jax.ShapeDtypeStruct((DEPTH, 2, nseq, GLA_HEADS, kw, vw), F32))
    n_chunks = seq_len // GLA_CHUNK
    scratch = [
        pltpu.VMEM((seq_len, kw), BF16),
        pltpu.VMEM((seq_len, kw), BF16),
        pltpu.VMEM((seq_len, vw), F32),
        pltpu.VMEM((2 * n_chunks, vw, kw), F32),
        pltpu.VMEM((2 * n_chunks * 8, kw), F32),
    ]
    return pl.pallas_call(
        functools.partial(_gla_kernel, seq_len=seq_len, has_s0=has_s0, emit_state=emit_state,
                          alias_state=alias_state),
        grid=(nseq, GLA_HEADS),
        in_specs=in_specs,
        out_specs=out_specs,
        out_shape=out_shape,
        scratch_shapes=scratch,
        input_output_aliases=aliases,
        compiler_params=_cparams(("arbitrary", "arbitrary")),
        name="gla_state" if emit_state else "gla",
    )(*args)


def _window_count(idx, n, w):
    lo = jnp.maximum(idx - w // 2, 0)
    hi = jnp.minimum(idx - w // 2 + w, n)
    return hi - lo


def _pool_project(pooled, gi, w_ref, sc_ref, o_ref, rows):
    cols = slice(gi * POOL_GROUP_DIM, (gi + 1) * POOL_GROUP_DIM)
    y = _dot(pooled.astype(BF16), w_ref[gi]) * sc_ref[:, cols]
    o_ref[rows, cols] = y.astype(BF16)


def _pool1d_kernel(p_ref, band_ref, w_ref, sc_ref, o_ref, *, seq_len):
    t = lax.broadcasted_iota(jnp.int32, (seq_len, 1), 0)
    for gi, w in enumerate(POOL_WINDOWS):
        x = p_ref[:, gi * POOL_GROUP_DIM:(gi + 1) * POOL_GROUP_DIM]
        cnt = _window_count(t, seq_len, w).astype(F32)
        _pool_project(_band_sum(band_ref[gi], x) / cnt - x, gi, w_ref, sc_ref, o_ref, slice(None))


def _pool2d_kernel(p_ref, band_ref, w_ref, sc_ref, o_ref, cs_ref, *, seq_len):
    n_rows = seq_len // GRID_W
    pad = (max(POOL_WINDOWS) // 2) * GRID_W
    n_blk = seq_len // POOL_BLK
    zeros = jnp.zeros((pad, POOL_GROUP_DIM), F32)
    cs_ref[pl.ds(0, pad), :] = zeros
    cs_ref[pl.ds(pad + seq_len, pad), :] = zeros
    local = lax.broadcasted_iota(jnp.int32, (POOL_BLK, 1), 0)
    col = local & (GRID_W - 1)
    for gi, w in enumerate(POOL_WINDOWS):
        cols = slice(gi * POOL_GROUP_DIM, (gi + 1) * POOL_GROUP_DIM)

        def col_sums(i, carry, gi=gi, cols=cols):
            r = pl.multiple_of(i * POOL_BLK, POOL_BLK)
            acc = _band_sum(band_ref[gi], p_ref[pl.ds(r, POOL_BLK), cols])
            cs_ref[pl.ds(pl.multiple_of(r + pad, GRID_W), POOL_BLK), :] = acc
            return carry

        lax.fori_loop(0, n_blk, col_sums, 0, unroll=2)

        def row_sums(i, carry, w=w, cols=cols, gi=gi):
            r = pl.multiple_of(i * POOL_BLK, POOL_BLK)
            acc = jnp.zeros((POOL_BLK, POOL_GROUP_DIM), F32)
            for j in range(w):
                off = pad + (j - w // 2) * GRID_W
                acc = acc + cs_ref[pl.ds(pl.multiple_of(r + off, GRID_W), POOL_BLK), :]
            grow = (r + local) >> GRID_SHIFT
            cnt = (_window_count(grow, n_rows, w) * _window_count(col, GRID_W, w)).astype(F32)
            x = p_ref[pl.ds(r, POOL_BLK), cols]
            _pool_project(acc / cnt - x, gi, w_ref, sc_ref, o_ref, pl.ds(r, POOL_BLK))
            return carry

        lax.fori_loop(0, n_blk, row_sums, 0, unroll=2)


def _band_matrices(period):
    i = np.arange(POOL_BLK)
    same = (i[:, None] // period) == (i[None, :] // period)
    d = i[None, :] - i[:, None]
    mats = [np.logical_and(same, np.logical_and(d >= -(w // 2), d < w - w // 2)) for w in POOL_WINDOWS]
    return jnp.asarray(np.stack(mats).astype(np.float32), BF16)


def _pool_call(z, w_pool, pool_scale, layer, nseq, seq_len, on_grid):
    m = nseq * seq_len
    if on_grid:
        body = functools.partial(_pool2d_kernel, seq_len=seq_len)
        pad = (max(POOL_WINDOWS) // 2) * GRID_W
        scratch = [pltpu.VMEM((seq_len + 2 * pad, POOL_GROUP_DIM), F32)]
        band = _band_matrices(GRID_W)
    else:
        assert seq_len == POOL_BLK, "1-D pooling handles one POOL_BLK-token sequence per grid step"
        body = functools.partial(_pool1d_kernel, seq_len=seq_len)
        scratch = []
        band = _band_matrices(POOL_BLK)
    n_groups = len(POOL_WINDOWS)
    return pl.pallas_call(
        body,
        grid=(nseq,),
        in_specs=[
            pl.BlockSpec((seq_len, POOL_WIDTH), lambda s: (s, Z_P // POOL_WIDTH)),
            pl.BlockSpec((n_groups, POOL_BLK, POOL_BLK), lambda s: (0, 0, 0)),
            pl.BlockSpec((None, n_groups, POOL_GROUP_DIM, POOL_GROUP_DIM), lambda s: (layer, 0, 0, 0)),
            _layer_vec_spec(layer, POOL_WIDTH),
        ],
        out_specs=pl.BlockSpec((seq_len, POOL_WIDTH), lambda s: (s, 0)),
        out_shape=jax.ShapeDtypeStruct((m, POOL_WIDTH), BF16),
        scratch_shapes=scratch,
        compiler_params=_cparams(("arbitrary",)),
        name="pool2d" if on_grid else "pool1d",
    )(z, band, w_pool, pool_scale)


def _mm_out_ln_kernel(mo_ref, mp_ref, w_ref, x_ref, g1_ref, sh2_ref, sc2_ref, lg_ref, lb_ref,
                      x1_ref, h2_ref):
    for s in range(x_ref.shape[0] // LN_SUB):
        rows = pl.ds(s * LN_SUB, LN_SUB)
        m = _dot(mo_ref[rows, :], w_ref[:GLA_WIDTH, :]) + _dot(mp_ref[rows, :], w_ref[GLA_WIDTH:, :])
        x1 = _layer_norm(ALPHA * x_ref[rows, :] + g1_ref[...] * m, lg_ref[...], lb_ref[...])
        x1_ref[rows, :] = x1
        h2_ref[rows, :] = (x1 * (1.0 + sc2_ref[...]) + sh2_ref[...]).astype(BF16)


def _mm_out_ln_call(mo, mp, w_out, x, mod4, rows_per_mod, ln_g, ln_b, layer):
    m = x.shape[0]
    tm = min(512, m)
    row = pl.BlockSpec((tm, D_MODEL), lambda i: (i, 0))
    half = pl.BlockSpec((tm, GLA_WIDTH), lambda i: (i, 0))
    w_spec = pl.BlockSpec((None, D_MODEL, D_MODEL), lambda i: (layer, 0, 0), pipeline_mode=pl.Buffered(1))
    mod = lambda part: _mod_spec(layer, part, rows_per_mod, tm)
    return pl.pallas_call(
        _mm_out_ln_kernel,
        grid=(m // tm,),
        in_specs=[half, half, w_spec, row, mod(2), mod(3), mod(4),
                  _layer_vec_spec(layer, D_MODEL), _layer_vec_spec(layer, D_MODEL)],
        out_specs=[row, row],
        out_shape=[jax.ShapeDtypeStruct((m, D_MODEL), F32), jax.ShapeDtypeStruct((m, D_MODEL), BF16)],
        compiler_params=_cparams(("arbitrary",)),
        name="mm_out_ln",
    )(mo, mp, w_out, x, mod4, mod4, mod4, ln_g, ln_b)


def _mm_gu_kernel(h_ref, wg_ref, wu_ref, o_ref):
    h = h_ref[...]
    gate = _dot(h, wg_ref[...])
    up = _dot(h, wu_ref[...])
    o_ref[...] = (jax.nn.silu(gate) * up).astype(BF16)


def _mm_gu_call(h, w_gu, layer):
    m = h.shape[0]
    tm = min(1024, m)
    tn = 512
    nj = D_FF // tn
    return pl.pallas_call(
        _mm_gu_kernel,
        grid=(nj, m // tm),
        in_specs=[
            pl.BlockSpec((tm, D_MODEL), lambda j, i: (i, 0)),
            pl.BlockSpec((None, D_MODEL, tn), lambda j, i: (layer, 0, j)),
            pl.BlockSpec((None, D_MODEL, tn), lambda j, i: (layer, 0, nj + j)),
        ],
        out_specs=pl.BlockSpec((tm, tn), lambda j, i: (i, j)),
        out_shape=jax.ShapeDtypeStruct((m, D_FF), BF16),
        compiler_params=_cparams(("arbitrary", "arbitrary")),
        name="mm_gu",
    )(h, w_gu, w_gu)


def _mm_down_ln_kernel(*refs, emit_h):
    a_ref, w_ref, x1_ref, g2_ref = refs[:4]
    pos = 4
    if emit_h:
        shn_ref, scn_ref = refs[pos:pos + 2]
        pos += 2
    lg_ref, lb_ref, x_ref = refs[pos:pos + 3]
    pos += 3
    if emit_h:
        h_ref = refs[pos]
    for s in range(x_ref.shape[0] // LN_SUB):
        rows = pl.ds(s * LN_SUB, LN_SUB)
        f = _dot(a_ref[rows, :], w_ref[...])
        x = _layer_norm(ALPHA * x1_ref[rows, :] + g2_ref[...] * f, lg_ref[...], lb_ref[...])
        x_ref[rows, :] = x
        if emit_h:
            h_ref[rows, :] = (x * (1.0 + scn_ref[...]) + shn_ref[...]).astype(BF16)


def _mm_down_ln_call(act, w_down, x1, mod4, rows_per_mod, ln_g, ln_b, layer):
    m = x1.shape[0]
    tm = min(256, m)
    emit_h = layer + 1 < DEPTH
    row = pl.BlockSpec((tm, D_MODEL), lambda i: (i, 0))
    in_specs = [
        pl.BlockSpec((tm, D_FF), lambda i: (i, 0)),
        pl.BlockSpec((None, D_FF, D_MODEL), lambda i: (layer, 0, 0), pipeline_mode=pl.Buffered(1)),
        row,
        _mod_spec(layer, 5, rows_per_mod, tm),
    ]
    args = [act, w_down, x1, mod4]
    if emit_h:
        in_specs += [_mod_spec(layer + 1, 0, rows_per_mod, tm), _mod_spec(layer + 1, 1, rows_per_mod, tm)]
        args += [mod4, mod4]
    in_specs += [_layer_vec_spec(layer, D_MODEL), _layer_vec_spec(layer, D_MODEL)]
    args += [ln_g, ln_b]
    out_specs = [row]
    out_shape = [jax.ShapeDtypeStruct((m, D_MODEL), F32)]
    if emit_h:
        out_specs.append(row)
        out_shape.append(jax.ShapeDtypeStruct((m, D_MODEL), BF16))
    outs = pl.pallas_call(
        functools.partial(_mm_down_ln_kernel, emit_h=emit_h),
        grid=(m // tm,),
        in_specs=in_specs,
        out_specs=out_specs,
        out_shape=out_shape,
        compiler_params=_cparams(("arbitrary",)),
        name="mm_down_ln",
    )(*args)
    return (outs[0], outs[1]) if emit_h else (outs[0], None)


def _repack_w_in_kernel(w_ref, o_ref):
    w = w_ref[...]
    rows = w.shape[0]
    o_ref[:, :Z_P] = w[:, :W_AF].astype(BF16)
    o_ref[:, Z_P:Z_GATE] = w[:, W_P:].astype(BF16)
    gate = jnp.concatenate(
        [w[:, W_AF:W_P], jnp.zeros((rows, Z_GATE_PAD - 2 * GLA_GATE_RANK), w.dtype)], axis=-1)
    o_ref[:, Z_GATE:] = gate.astype(BF16)


def _repack_w_in_call(w_in):
    depth, k, n = w_in.shape
    tk = 256
    return pl.pallas_call(
        _repack_w_in_kernel,
        grid=(depth, k // tk),
        in_specs=[pl.BlockSpec((None, tk, n), lambda l, i: (l, i, 0))],
        out_specs=pl.BlockSpec((None, tk, Z_WIDTH), lambda l, i: (l, i, 0)),
        out_shape=jax.ShapeDtypeStruct((depth, k, Z_WIDTH), BF16),
        compiler_params=_cparams(("arbitrary", "arbitrary")),
        name="repack_w_in",
    )(w_in)


def _prep_weights(w_in, w_a2, b_a, w_pool, w_out, w_gu, w_down):
    depth = w_in.shape[0]
    w_in_r = _repack_w_in_call(w_in)
    wgate = jnp.zeros((depth, Z_GATE_PAD, GLA_HEADS, 2, GLA_DK), F32)
    per_head = lambda t: t.reshape(depth, GLA_GATE_RANK, GLA_HEADS, GLA_DK)
    wgate = wgate.at[:, :GLA_GATE_RANK, :, 0, :].set(per_head(w_a2[:, 0]))
    wgate = wgate.at[:, GLA_GATE_RANK:2 * GLA_GATE_RANK, :, 1, :].set(per_head(w_a2[:, 1]))
    wgate = wgate.reshape(depth, Z_GATE_PAD, 2 * GLA_KEY_WIDTH)
    bgate = b_a.reshape(depth, 2, GLA_HEADS, GLA_DK).transpose(0, 2, 1, 3).reshape(depth, 1, 2 * GLA_KEY_WIDTH)
    return (w_in_r, wgate.astype(BF16), bgate, w_pool.astype(BF16), w_out.astype(BF16),
            w_gu.astype(BF16), w_down.astype(BF16))


def kernel(x_prompt, x_sample, state_gla, c, c_ctx, ln_in_g, ln_in_b, w_ada, b_ada, w_in, w_a2, b_a,
           gla_norm_g, w_pool, pool_scale, w_out, ln1_g, ln1_b, w_gu, w_down, ln2_g, ln2_b):
    nb_p, len_p, _ = x_prompt.shape
    nb_s, len_s, _ = x_sample.shape
    w_in_r, wgate, bgate, w_pool_b, w_out_b, w_gu_b, w_down_b = _prep_weights(
        w_in, w_a2, b_a, w_pool, w_out, w_gu, w_down)
    gnorm = gla_norm_g.reshape(DEPTH, 1, GLA_WIDTH)
    pscale = pool_scale.reshape(DEPTH, 1, POOL_WIDTH)
    ln1 = (ln1_g.reshape(DEPTH, 1, D_MODEL), ln1_b.reshape(DEPTH, 1, D_MODEL))
    ln2 = (ln2_g.reshape(DEPTH, 1, D_MODEL), ln2_b.reshape(DEPTH, 1, D_MODEL))

    n_mod = -(-(1 + nb_s) // 8) * 8
    c_all = jnp.concatenate(
        [c_ctx[None, :], c, jnp.zeros((n_mod - 1 - nb_s, D_MODEL), c.dtype)], axis=0)
    mod4 = _ada_call(c_all, w_ada, b_ada).reshape(DEPTH, n_mod, 1, 6 * D_MODEL)

    streams = [
        (x_prompt.reshape(nb_p * len_p, D_MODEL), None, nb_p, len_p, False),
        (x_sample.reshape(nb_s * len_s, D_MODEL), len_s, nb_s, len_s, True),
    ]
    results = []
    state_buf = None
    for x_in, rows_per_mod, nseq, seq_len, on_grid in streams:
        x, h = _ln_in_call(x_in, ln_in_g, ln_in_b, mod4, rows_per_mod)
        for l in range(DEPTH):
            z = _mm_in_call(h, w_in_r, l)
            if on_grid:
                (mo,) = _gla_call(z, wgate, bgate, gnorm, l, nseq, seq_len, s0=state_gla)
            else:
                mo, state_buf = _gla_call(z, wgate, bgate, gnorm, l, nseq, seq_len,
                                          emit_state=True, state_buf=state_buf)
            mp = _pool_call(z, w_pool_b, pscale, l, nseq, seq_len, on_grid)
            x1, h2 = _mm_out_ln_call(mo, mp, w_out_b, x, mod4, rows_per_mod, ln1[0], ln1[1], l)
            act = _mm_gu_call(h2, w_gu_b, l)
            x, h = _mm_down_ln_call(act, w_down_b, x1, mod4, rows_per_mod, ln2[0], ln2[1], l)
        results.append(x)
    y_prompt = results[0].reshape(nb_p, len_p, D_MODEL)
    y_sample = results[1].reshape(nb_s, len_s, D_MODEL)
    return (y_prompt, y_sample, state_buf.astype(x_prompt.dtype))
```

```python
import functools

import numpy as np
import jax
import jax.numpy as jnp
from jax import lax
from jax.experimental import pallas as pl
from jax.experimental.pallas import tpu as pltpu

F32 = jnp.float32
BF16 = jnp.bfloat16

D_MODEL = 2048
DEPTH = 4
GLA_HEADS = 4
GLA_DV = 256
GLA_DK = 128
GLA_KEY_WIDTH = GLA_HEADS * GLA_DK
GLA_WIDTH = GLA_HEADS * GLA_DV
POOL_WIDTH = 1024
GLA_GATE_RANK = 16
GLA_GATE_TEMP = 16.0
GLA_CHUNK = 64
POOL_WINDOWS = (2, 4, 8, 16)
POOL_GROUP_DIM = 256
GRID_W = 64
D_FF = 5632
ALPHA = (2 * DEPTH) ** 0.25
LN_EPS = 1e-5
Q_SCALE = GLA_DK ** -0.5

Z_Q = 0
Z_K = Z_Q + GLA_KEY_WIDTH
Z_V = Z_K + GLA_KEY_WIDTH
Z_G = Z_V + GLA_WIDTH
Z_P = Z_G + GLA_WIDTH
Z_GATE = Z_P + POOL_WIDTH
Z_GATE_PAD = 128
Z_WIDTH = Z_GATE + Z_GATE_PAD
W_AF = 2 * GLA_KEY_WIDTH + 2 * GLA_WIDTH
W_P = W_AF + 2 * GLA_GATE_RANK

P1_BLK = 256
POOL_BLK = 256
LN_SUB = 128
CHUNK_SHIFT = GLA_CHUNK.bit_length() - 1
GRID_SHIFT = GRID_W.bit_length() - 1
V7X_VMEM_LIMIT_BYTES = 56 * 2 ** 20


def _log_sigmoid(x):
    return jnp.minimum(x, 0.0) - jnp.log1p(jnp.exp(-jnp.abs(x)))


def _dot(a, b):
    return jnp.dot(a, b, preferred_element_type=F32)


def _dot_nt(a, b):
    return lax.dot_general(a, b, (((1,), (1,)), ((), ())), preferred_element_type=F32)


def _band_sum(band, x):
    hi = x.astype(BF16)
    r1 = x - hi.astype(F32)
    mid = r1.astype(BF16)
    lo = (r1 - mid.astype(F32)).astype(BF16)
    return _dot(band, hi) + _dot(band, mid) + _dot(band, lo)


def _cparams(semantics):
    return pltpu.CompilerParams(dimension_semantics=semantics, vmem_limit_bytes=V7X_VMEM_LIMIT_BYTES)


def _layer_norm(y, g, b):
    mu = jnp.mean(y, axis=-1, keepdims=True)
    yc = y - mu
    var = jnp.mean(yc * yc, axis=-1, keepdims=True)
    return yc * lax.rsqrt(var + LN_EPS) * g + b


def _mod_spec(layer, part, rows_per_mod, tm):
    if rows_per_mod is None:
        return pl.BlockSpec((None, None, 1, D_MODEL), lambda i, *_: (layer, 0, 0, part))
    return pl.BlockSpec((None, None, 1, D_MODEL),
                        lambda i, *_: (layer, 1 + (i * tm) // rows_per_mod, 0, part))


def _layer_vec_spec(layer, width):
    return pl.BlockSpec((None, 1, width), lambda *_: (layer, 0, 0))


def _ada_kernel(c_ref, w_ref, b_ref, o_ref):
    c = c_ref[...]
    s = jax.nn.silu(c).astype(BF16)
    o_ref[...] = _dot(s, w_ref[...].astype(BF16)) + b_ref[...]


def _ada_call(c_all, w_ada, b_ada):
    rows = c_all.shape[0]
    tn = 1024
    nj = (6 * D_MODEL) // tn
    return pl.pallas_call(
        _ada_kernel,
        grid=(DEPTH, nj),
        in_specs=[
            pl.BlockSpec((rows, D_MODEL), lambda l, j: (0, 0)),
            pl.BlockSpec((None, D_MODEL, tn), lambda l, j: (l, 0, j)),
            pl.BlockSpec((None, 1, tn), lambda l, j: (l, 0, j)),
        ],
        out_specs=pl.BlockSpec((None, rows, tn), lambda l, j: (l, 0, j)),
        out_shape=jax.ShapeDtypeStruct((DEPTH, rows, 6 * D_MODEL), F32),
        compiler_params=_cparams(("arbitrary", "arbitrary")),
        name="ada_mod",
    )(c_all, w_ada, b_ada.reshape(DEPTH, 1, 6 * D_MODEL))


def _ln_in_kernel(x_ref, g_ref, b_ref, sh_ref, sc_ref, x0_ref, h_ref):
    y = _layer_norm(x_ref[...], g_ref[...], b_ref[...])
    x0_ref[...] = y
    h_ref[...] = (y * (1.0 + sc_ref[...]) + sh_ref[...]).astype(BF16)


def _ln_in_call(x, g, b, mod4, rows_per_mod):
    m = x.shape[0]
    tm = min(512, m)
    row = pl.BlockSpec((tm, D_MODEL), lambda i: (i, 0))
    vec = pl.BlockSpec((1, D_MODEL), lambda i: (0, 0))
    return pl.pallas_call(
        _ln_in_kernel,
        grid=(m // tm,),
        in_specs=[row, vec, vec, _mod_spec(0, 0, rows_per_mod, tm), _mod_spec(0, 1, rows_per_mod, tm)],
        out_specs=[row, row],
        out_shape=[jax.ShapeDtypeStruct((m, D_MODEL), F32), jax.ShapeDtypeStruct((m, D_MODEL), BF16)],
        compiler_params=_cparams(("arbitrary",)),
        name="ln_in",
    )(x, g.reshape(1, D_MODEL), b.reshape(1, D_MODEL), mod4, mod4)


def _mm_kernel(h_ref, w_ref, o_ref):
    o_ref[...] = _dot(h_ref[...], w_ref[...])


def _mm_in_call(h, w, layer):
    m = h.shape[0]
    tm = min(512, m)
    return pl.pallas_call(
        _mm_kernel,
        grid=(m // tm,),
        in_specs=[
            pl.BlockSpec((tm, D_MODEL), lambda i: (i, 0)),
            pl.BlockSpec((None, D_MODEL, Z_WIDTH), lambda i: (layer, 0, 0), pipeline_mode=pl.Buffered(1)),
        ],
        out_specs=pl.BlockSpec((tm, Z_WIDTH), lambda i: (i, 0)),
        out_shape=jax.ShapeDtypeStruct((m, Z_WIDTH), F32),
        compiler_params=_cparams(("arbitrary",)),
        name="mm_in",
    )(h, w)


def _gla_kernel(*refs, seq_len, has_s0, emit_state, alias_state):
    (q_ref, k_ref, v_ref, g_ref, zg_ref, wg_ref, bg_ref, gn_ref, tl_ref) = refs[:9]
    pos = 9
    if has_s0:
        s0f_ref, s0b_ref = refs[pos:pos + 2]
        pos += 2
    if alias_state:
        pos += 1
    mo_ref = refs[pos]
    pos += 1
    if emit_state:
        st_ref = refs[pos]
        pos += 1
    qef, qeb, o_ref, u_ref, dec_ref = refs[pos:]

    n_blk = seq_len // P1_BLK
    n_chunks = seq_len // GLA_CHUNK
    c = GLA_CHUNK
    dk = GLA_DK

    brow = lax.broadcasted_iota(jnp.int32, (P1_BLK, P1_BLK), 0)
    bcol = lax.broadcasted_iota(jnp.int32, (P1_BLK, P1_BLK), 1)
    same_chunk = (brow >> CHUNK_SHIFT) == (bcol >> CHUNK_SHIFT)
    mask_f = jnp.logical_and(same_chunk, bcol <= brow)
    mask_b = jnp.logical_and(same_chunk, bcol >= brow)
    row_chunk = lax.broadcasted_iota(jnp.int32, (P1_BLK, dk), 0) >> CHUNK_SHIFT

    def phase_a(i, carry):
        r = pl.multiple_of(i * P1_BLK, P1_BLK)
        rows = pl.ds(r, P1_BLK)
        zg = zg_ref[rows, :].astype(BF16)
        la = _log_sigmoid(_dot(zg, wg_ref[...]) + bg_ref[...]) * (1.0 / GLA_GATE_TEMP)
        pre = _band_sum(tl_ref[...], la)
        n_sub = P1_BLK // c
        tot = jnp.concatenate(
            [jnp.broadcast_to(pre[(s + 1) * c - 1:(s + 1) * c, :], (c, 2 * dk)) for s in range(n_sub)], axis=0)
        b_f = pre[:, :dk]
        tot_f = tot[:, :dk]
        tot_b = tot[:, dk:]
        b_b = tot_b - pre[:, dk:] + la[:, dk:]
        q = q_ref[rows, :] * Q_SCALE
        k = k_ref[rows, :]
        qe_f = (q * jnp.exp(b_f)).astype(BF16)
        qe_b = (q * jnp.exp(b_b)).astype(BF16)
        qef[rows, :] = qe_f
        qeb[rows, :] = qe_b
        ke_f = (k * jnp.exp(-b_f)).astype(BF16)
        ke_b = (k * jnp.exp(-b_b)).astype(BF16)
        a = (jnp.where(mask_f, _dot_nt(qe_f, ke_f), 0.0) + jnp.where(mask_b, _dot_nt(qe_b, ke_b), 0.0))
        v = v_ref[rows, :]
        o_ref[rows, :] = _dot(a.astype(BF16), v.astype(BF16))
        kd_f = k * jnp.exp(tot_f - b_f)
        kd_b = k * jnp.exp(tot_b - b_b)
        kd_cols = [jnp.where(row_chunk == s, kd, 0.0).astype(BF16) for kd in (kd_f, kd_b) for s in range(n_sub)]
        u = _dot(v.T.astype(BF16), jnp.concatenate(kd_cols, axis=-1))
        for s in range(n_sub):
            ch = n_sub * i + s
            u_ref[ch] = u[:, s * dk:(s + 1) * dk]
            u_ref[n_chunks + ch] = u[:, (n_sub + s) * dk:(n_sub + s + 1) * dk]
            dec = jnp.exp(tot[s * c:s * c + 8, :])
            dec_ref[pl.ds(pl.multiple_of(8 * ch, 8), 8), :] = dec[:, :dk]
            dec_ref[pl.ds(pl.multiple_of(8 * (n_chunks + ch), 8), 8), :] = dec[:, dk:]
        return carry

    lax.fori_loop(0, n_blk, phase_a, 0, unroll=2 if n_blk % 2 == 0 else 1)

    def fwd_chunk(ch, st):
        rows = pl.ds(pl.multiple_of(ch * c, c), c)
        o_ref[rows, :] += _dot_nt(qef[rows, :], st.astype(BF16))
        return st * dec_ref[pl.ds(pl.multiple_of(8 * ch, 8), 1), :] + u_ref[ch]

    def bwd_chunk(j, st):
        ch = n_chunks - 1 - j
        rows = pl.ds(pl.multiple_of(ch * c, c), c)
        o = o_ref[rows, :] + _dot_nt(qeb[rows, :], st.astype(BF16))
        ms = jnp.mean(o * o, axis=-1, keepdims=True)
        y = o * lax.rsqrt(ms + LN_EPS) * gn_ref[...]
        mo_ref[rows, :] = (y * jax.nn.silu(g_ref[rows, :])).astype(BF16)
        return st * dec_ref[pl.ds(pl.multiple_of(8 * (n_chunks + ch), 8), 1), :] + u_ref[n_chunks + ch]

    if has_s0:
        st0_f = s0f_ref[...].T
        st0_b = s0b_ref[...].T
    else:
        st0_f = jnp.zeros((GLA_DV, dk), F32)
        st0_b = st0_f
    unroll = True if n_chunks <= 4 else 4
    st_f = lax.fori_loop(0, n_chunks, fwd_chunk, st0_f, unroll=unroll)
    st_b = lax.fori_loop(0, n_chunks, bwd_chunk, st0_b, unroll=unroll)
    if emit_state:
        st_ref[0] = st_f.T
        st_ref[1] = st_b.T


def _chunk_prefix_matrix():
    i = np.arange(P1_BLK)
    same = (i[:, None] >> CHUNK_SHIFT) == (i[None, :] >> CHUNK_SHIFT)
    return jnp.asarray(np.logical_and(same, i[None, :] <= i[:, None]).astype(np.float32), BF16)


def _gla_call(z, wgate, bgate, gnorm, layer, nseq, seq_len, s0=None, emit_state=False, state_buf=None):
    m = nseq * seq_len
    tl = _chunk_prefix_matrix()
    has_s0 = s0 is not None
    alias_state = state_buf is not None
    kw, vw = GLA_DK, GLA_DV
    in_specs = [
        pl.BlockSpec((seq_len, kw), lambda s, h: (s, Z_Q // kw + h)),
        pl.BlockSpec((seq_len, kw), lambda s, h: (s, Z_K // kw + h)),
        pl.BlockSpec((seq_len, vw), lambda s, h: (s, Z_V // vw + h)),
        pl.BlockSpec((seq_len, vw), lambda s, h: (s, Z_G // vw + h)),
        pl.BlockSpec((seq_len, Z_GATE_PAD), lambda s, h: (s, Z_GATE // Z_GATE_PAD)),
        pl.BlockSpec((None, Z_GATE_PAD, 2 * kw), lambda s, h: (layer, 0, h)),
        pl.BlockSpec((None, 1, 2 * kw), lambda s, h: (layer, 0, h)),
        pl.BlockSpec((None, 1, vw), lambda s, h: (layer, 0, h)),
        pl.BlockSpec((P1_BLK, P1_BLK), lambda s, h: (0, 0)),
    ]
    args = [z, z, z, z, z, wgate, bgate, gnorm, tl]
    if has_s0:
        in_specs += [
            pl.BlockSpec((None, None, None, None, kw, vw), lambda s, h: (layer, 0, s, h, 0, 0)),
            pl.BlockSpec((None, None, None, None, kw, vw), lambda s, h: (layer, 1, s, h, 0, 0)),
        ]
        args += [s0, s0]
    aliases = {}
    if alias_state:
        aliases = {len(args): 1}
        in_specs.append(pl.BlockSpec(memory_space=pl.ANY))
        args.append(state_buf)
    out_specs = [pl.BlockSpec((seq_len, vw), lambda s, h: (s, h))]
    out_shape = [jax.ShapeDtypeStruct((m, GLA_WIDTH), BF16)]
    if emit_state:
        out_specs.append(pl.BlockSpec((None, 2, None, None, kw, vw), lambda s, h: (layer, 0, s, h, 0, 0)))
        out_shape.append(jax.ShapeDtypeStruct((DEPTH, 2, nseq, GLA_HEADS, kw, vw), F32))
    n_chunks = seq_len // GLA_CHUNK
    scratch = [
        pltpu.VMEM((seq_len, kw), BF16),
        pltpu.VMEM((seq_len, kw), BF16),
        pltpu.VMEM((seq_len, vw), F32),
        pltpu.VMEM((2 * n_chunks, vw, kw), F32),
        pltpu.VMEM((2 * n_chunks * 8, kw), F32),
    ]
    return pl.pallas_call(
        functools.partial(_gla_kernel, seq_len=seq_len, has_s0=has_s0, emit_state=emit_state,
                          alias_state=alias_state),
        grid=(nseq, GLA_HEADS),
        in_specs=in_specs,
        out_specs=out_specs,
        out_shape=out_shape,
        scratch_shapes=scratch,
        input_output_aliases=aliases,
        compiler_params=_cparams(("arbitrary", "arbitrary")),
        name="gla_state" if emit_state else "gla",
    )(*args)


def _window_count(idx, n, w):
    lo = jnp.maximum(idx - w // 2, 0)
    hi = jnp.minimum(idx - w // 2 + w, n)
    return hi - lo


def _pool_project(pooled, gi, w_ref, sc_ref, o_ref, rows):
    cols = slice(gi * POOL_GROUP_DIM, (gi + 1) * POOL_GROUP_DIM)
    y = _dot(pooled.astype(BF16), w_ref[gi]) * sc_ref[:, cols]
    o_ref[rows, cols] = y.astype(BF16)


def _shifted(x, pos, n, d):
    rolled = pltpu.roll(x, (-d) % x.shape[0], 0)
    return jnp.where(jnp.logical_and(pos + d >= 0, pos + d < n), rolled, 0.0)


def _window_sum(x, pos, n, w):
    fwd = bwd = x
    s = 1
    while s < w // 2:
        fwd = fwd + _shifted(fwd, pos, n, s)
        bwd = bwd + _shifted(bwd, pos, n, -s)
        s *= 2
    return fwd + _shifted(bwd, pos, n, -1)


def _pool1d_kernel(p_ref, w_ref, sc_ref, o_ref, *, seq_len):
    t = lax.broadcasted_iota(jnp.int32, (seq_len, 1), 0)
    for gi, w in enumerate(POOL_WINDOWS):
        x = p_ref[:, gi * POOL_GROUP_DIM:(gi + 1) * POOL_GROUP_DIM]
        cnt = _window_count(t, seq_len, w).astype(F32)
        _pool_project(_window_sum(x, t, seq_len, w) / cnt - x, gi, w_ref, sc_ref, o_ref, slice(None))


def _pool2d_kernel(p_ref, w_ref, sc_ref, o_ref, cs_ref, *, seq_len):
    n_rows = seq_len // GRID_W
    pad = (max(POOL_WINDOWS) // 2) * GRID_W
    n_blk = seq_len // POOL_BLK
    zeros = jnp.zeros((pad, POOL_GROUP_DIM), F32)
    cs_ref[pl.ds(0, pad), :] = zeros
    cs_ref[pl.ds(pad + seq_len, pad), :] = zeros
    local = lax.broadcasted_iota(jnp.int32, (POOL_BLK, 1), 0)
    col = local & (GRID_W - 1)
    for gi, w in enumerate(POOL_WINDOWS):
        cols = slice(gi * POOL_GROUP_DIM, (gi + 1) * POOL_GROUP_DIM)

        def col_sums(i, carry, w=w, cols=cols):
            r = pl.multiple_of(i * POOL_BLK, POOL_BLK)
            acc = _window_sum(p_ref[pl.ds(r, POOL_BLK), cols], col, GRID_W, w)
            cs_ref[pl.ds(pl.multiple_of(r + pad, GRID_W), POOL_BLK), :] = acc
            return carry

        lax.fori_loop(0, n_blk, col_sums, 0)

        def row_sums(i, carry, w=w, cols=cols, gi=gi):
            r = pl.multiple_of(i * POOL_BLK, POOL_BLK)
            acc = jnp.zeros((POOL_BLK, POOL_GROUP_DIM), F32)
            for j in range(w):
                off = pad + (j - w // 2) * GRID_W
                acc = acc + cs_ref[pl.ds(pl.multiple_of(r + off, GRID_W), POOL_BLK), :]
            grow = (r + local) >> GRID_SHIFT
            cnt = (_window_count(grow, n_rows, w) * _window_count(col, GRID_W, w)).astype(F32)
            x = p_ref[pl.ds(r, POOL_BLK), cols]
            _pool_project(acc / cnt - x, gi, w_ref, sc_ref, o_ref, pl.ds(r, POOL_BLK))
            return carry

        lax.fori_loop(0, n_blk, row_sums, 0, unroll=2)


def _pool_call(z, w_pool, pool_scale, layer, nseq, seq_len, on_grid):
    m = nseq * seq_len
    if on_grid:
        body = functools.partial(_pool2d_kernel, seq_len=seq_len)
        pad = (max(POOL_WINDOWS) // 2) * GRID_W
        scratch = [pltpu.VMEM((seq_len + 2 * pad, POOL_GROUP_DIM), F32)]
    else:
        body = functools.partial(_pool1d_kernel, seq_len=seq_len)
        scratch = []
    n_groups = len(POOL_WINDOWS)
    return pl.pallas_call(
        body,
        grid=(nseq,),
        in_specs=[
            pl.BlockSpec((seq_len, POOL_WIDTH), lambda s: (s, Z_P // POOL_WIDTH)),
            pl.BlockSpec((None, n_groups, POOL_GROUP_DIM, POOL_GROUP_DIM), lambda s: (layer, 0, 0, 0)),
            _layer_vec_spec(layer, POOL_WIDTH),
        ],
        out_specs=pl.BlockSpec((seq_len, POOL_WIDTH), lambda s: (s, 0)),
        out_shape=jax.ShapeDtypeStruct((m, POOL_WIDTH), BF16),
        scratch_shapes=scratch,
        compiler_params=_cparams(("arbitrary",)),
        name="pool2d" if on_grid else "pool1d",
    )(z, w_pool, pool_scale)


def _mm_out_ln_kernel(mo_ref, mp_ref, w_ref, x_ref, g1_ref, sh2_ref, sc2_ref, lg_ref, lb_ref,
                      x1_ref, h2_ref):
    for s in range(x_ref.shape[0] // LN_SUB):
        rows = pl.ds(s * LN_SUB, LN_SUB)
        m = _dot(mo_ref[rows, :], w_ref[:GLA_WIDTH, :]) + _dot(mp_ref[rows, :], w_ref[GLA_WIDTH:, :])
        x1 = _layer_norm(ALPHA * x_ref[rows, :] + g1_ref[...] * m, lg_ref[...], lb_ref[...])
        x1_ref[rows, :] = x1
        h2_ref[rows, :] = (x1 * (1.0 + sc2_ref[...]) + sh2_ref[...]).astype(BF16)


def _mm_out_ln_call(mo, mp, w_out, x, mod4, rows_per_mod, ln_g, ln_b, layer):
    m = x.shape[0]
    tm = min(512, m)
    row = pl.BlockSpec((tm, D_MODEL), lambda i: (i, 0))
    half = pl.BlockSpec((tm, GLA_WIDTH), lambda i: (i, 0))
    w_spec = pl.BlockSpec((None, D_MODEL, D_MODEL), lambda i: (layer, 0, 0), pipeline_mode=pl.Buffered(1))
    mod = lambda part: _mod_spec(layer, part, rows_per_mod, tm)
    return pl.pallas_call(
        _mm_out_ln_kernel,
        grid=(m // tm,),
        in_specs=[half, half, w_spec, row, mod(2), mod(3), mod(4),
                  _layer_vec_spec(layer, D_MODEL), _layer_vec_spec(layer, D_MODEL)],
        out_specs=[row, row],
        out_shape=[jax.ShapeDtypeStruct((m, D_MODEL), F32), jax.ShapeDtypeStruct((m, D_MODEL), BF16)],
        compiler_params=_cparams(("arbitrary",)),
        name="mm_out_ln",
    )(mo, mp, w_out, x, mod4, mod4, mod4, ln_g, ln_b)


def _mm_gu_kernel(h_ref, wg_ref, wu_ref, o_ref):
    h = h_ref[...]
    gate = _dot(h, wg_ref[...].astype(BF16))
    up = _dot(h, wu_ref[...].astype(BF16))
    o_ref[...] = (jax.nn.silu(gate) * up).astype(BF16)


def _mm_gu_call(h, w_gu, layer):
    m = h.shape[0]
    tm = min(1024, m)
    tn = 512
    nj = D_FF // tn
    return pl.pallas_call(
        _mm_gu_kernel,
        grid=(nj, m // tm),
        in_specs=[
            pl.BlockSpec((tm, D_MODEL), lambda j, i: (i, 0)),
            pl.BlockSpec((None, D_MODEL, tn), lambda j, i: (layer, 0, j)),
            pl.BlockSpec((None, D_MODEL, tn), lambda j, i: (layer, 0, nj + j)),
        ],
        out_specs=pl.BlockSpec((tm, tn), lambda j, i: (i, j)),
        out_shape=jax.ShapeDtypeStruct((m, D_FF), BF16),
        compiler_params=_cparams(("arbitrary", "arbitrary")),
        name="mm_gu",
    )(h, w_gu, w_gu)


def _mm_down_ln_kernel(*refs, emit_h):
    a_ref, w_ref, x1_ref, g2_ref = refs[:4]
    pos = 4
    if emit_h:
        shn_ref, scn_ref = refs[pos:pos + 2]
        pos += 2
    lg_ref, lb_ref, x_ref = refs[pos:pos + 3]
    pos += 3
    if emit_h:
        h_ref = refs[pos]
    for s in range(x_ref.shape[0] // LN_SUB):
        rows = pl.ds(s * LN_SUB, LN_SUB)
        f = _dot(a_ref[rows, :], w_ref[...])
        x = _layer_norm(ALPHA * x1_ref[rows, :] + g2_ref[...] * f, lg_ref[...], lb_ref[...])
        x_ref[rows, :] = x
        if emit_h:
            h_ref[rows, :] = (x * (1.0 + scn_ref[...]) + shn_ref[...]).astype(BF16)


def _mm_down_ln_call(act, w_down, x1, mod4, rows_per_mod, ln_g, ln_b, layer):
    m = x1.shape[0]
    tm = min(256, m)
    emit_h = layer + 1 < DEPTH
    row = pl.BlockSpec((tm, D_MODEL), lambda i: (i, 0))
    in_specs = [
        pl.BlockSpec((tm, D_FF), lambda i: (i, 0)),
        pl.BlockSpec((None, D_FF, D_MODEL), lambda i: (layer, 0, 0), pipeline_mode=pl.Buffered(1)),
        row,
        _mod_spec(layer, 5, rows_per_mod, tm),
    ]
    args = [act, w_down, x1, mod4]
    if emit_h:
        in_specs += [_mod_spec(layer + 1, 0, rows_per_mod, tm), _mod_spec(layer + 1, 1, rows_per_mod, tm)]
        args += [mod4, mod4]
    in_specs += [_layer_vec_spec(layer, D_MODEL), _layer_vec_spec(layer, D_MODEL)]
    args += [ln_g, ln_b]
    out_specs = [row]
    out_shape = [jax.ShapeDtypeStruct((m, D_MODEL), F32)]
    if emit_h:
        out_specs.append(row)
        out_shape.append(jax.ShapeDtypeStruct((m, D_MODEL), BF16))
    outs = pl.pallas_call(
        functools.partial(_mm_down_ln_kernel, emit_h=emit_h),
        grid=(m // tm,),
        in_specs=in_specs,
        out_specs=out_specs,
        out_shape=out_shape,
        compiler_params=_cparams(("arbitrary",)),
        name="mm_down_ln",
    )(*args)
    return (outs[0], outs[1]) if emit_h else (outs[0], None)


def _prep_weights(w_in, w_a2, b_a, w_pool, w_out, w_down):
    depth = w_in.shape[0]
    gate_pad = jnp.zeros((depth, D_MODEL, Z_GATE_PAD - 2 * GLA_GATE_RANK), w_in.dtype)
    w_in_r = jnp.concatenate(
        [w_in[:, :, :W_AF], w_in[:, :, W_P:], w_in[:, :, W_AF:W_P], gate_pad], axis=-1).astype(BF16)
    wgate = jnp.zeros((depth, Z_GATE_PAD, GLA_HEADS, 2, GLA_DK), F32)
    per_head = lambda t: t.reshape(depth, GLA_GATE_RANK, GLA_HEADS, GLA_DK)
    wgate = wgate.at[:, :GLA_GATE_RANK, :, 0, :].set(per_head(w_a2[:, 0]))
    wgate = wgate.at[:, GLA_GATE_RANK:2 * GLA_GATE_RANK, :, 1, :].set(per_head(w_a2[:, 1]))
    wgate = wgate.reshape(depth, Z_GATE_PAD, 2 * GLA_KEY_WIDTH)
    bgate = b_a.reshape(depth, 2, GLA_HEADS, GLA_DK).transpose(0, 2, 1, 3).reshape(depth, 1, 2 * GLA_KEY_WIDTH)
    return (w_in_r, wgate.astype(BF16), bgate, w_pool.astype(BF16), w_out.astype(BF16),
            w_down.astype(BF16))


def kernel(x_prompt, x_sample, state_gla, c, c_ctx, ln_in_g, ln_in_b, w_ada, b_ada, w_in, w_a2, b_a,
           gla_norm_g, w_pool, pool_scale, w_out, ln1_g, ln1_b, w_gu, w_down, ln2_g, ln2_b):
    nb_p, len_p, _ = x_prompt.shape
    nb_s, len_s, _ = x_sample.shape
    w_in_r, wgate, bgate, w_pool_b, w_out_b, w_down_b = _prep_weights(
        w_in, w_a2, b_a, w_pool, w_out, w_down)
    gnorm = gla_norm_g.reshape(DEPTH, 1, GLA_WIDTH)
    pscale = pool_scale.reshape(DEPTH, 1, POOL_WIDTH)
    ln1 = (ln1_g.reshape(DEPTH, 1, D_MODEL), ln1_b.reshape(DEPTH, 1, D_MODEL))
    ln2 = (ln2_g.reshape(DEPTH, 1, D_MODEL), ln2_b.reshape(DEPTH, 1, D_MODEL))

    n_mod = -(-(1 + nb_s) // 8) * 8
    c_all = jnp.concatenate(
        [c_ctx[None, :], c, jnp.zeros((n_mod - 1 - nb_s, D_MODEL), c.dtype)], axis=0)
    mod4 = _ada_call(c_all, w_ada, b_ada).reshape(DEPTH, n_mod, 1, 6 * D_MODEL)

    streams = [
        (x_prompt.reshape(nb_p * len_p, D_MODEL), None, nb_p, len_p, False),
        (x_sample.reshape(nb_s * len_s, D_MODEL), len_s, nb_s, len_s, True),
    ]
    results = []
    state_buf = None
    for x_in, rows_per_mod, nseq, seq_len, on_grid in streams:
        x, h = _ln_in_call(x_in, ln_in_g, ln_in_b, mod4, rows_per_mod)
        for l in range(DEPTH):
            z = _mm_in_call(h, w_in_r, l)
            if on_grid:
                (mo,) = _gla_call(z, wgate, bgate, gnorm, l, nseq, seq_len, s0=state_gla)
            else:
                mo, state_buf = _gla_call(z, wgate, bgate, gnorm, l, nseq, seq_len,
                                          emit_state=True, state_buf=state_buf)
            mp = _pool_call(z, w_pool_b, pscale, l, nseq, seq_len, on_grid)
            x1, h2 = _mm_out_ln_call(mo, mp, w_out_b, x, mod4, rows_per_mod, ln1[0], ln1[1], l)
            act = _mm_gu_call(h2, w_gu, l)
            x, h = _mm_down_ln_call(act, w_down_b, x1, mod4, rows_per_mod, ln2[0], ln2[1], l)
        results.append(x)
    y_prompt = results[0].reshape(nb_p, len_p, D_MODEL)
    y_sample = results[1].reshape(nb_s, len_s, D_MODEL)
    return (y_prompt, y_sample, state_buf.astype(x_prompt.dtype))
```

```python
import functools

import numpy as np
import jax
import jax.numpy as jnp
from jax import lax
from jax.experimental import pallas as pl
from jax.experimental.pallas import tpu as pltpu

F32 = jnp.float32
BF16 = jnp.bfloat16

D_MODEL = 2048
DEPTH = 4
GLA_HEADS = 4
GLA_DV = 256
GLA_DK = 128
GLA_KEY_WIDTH = GLA_HEADS * GLA_DK
GLA_WIDTH = GLA_HEADS * GLA_DV
POOL_WIDTH = 1024
GLA_GATE_RANK = 16
GLA_GATE_TEMP = 16.0
GLA_CHUNK = 64
POOL_WINDOWS = (2, 4, 8, 16)
POOL_GROUP_DIM = 256
GRID_W = 64
D_FF = 5632
ALPHA = (2 * DEPTH) ** 0.25
LN_EPS = 1e-5
Q_SCALE = GLA_DK ** -0.5

Z_Q = 0
Z_K = Z_Q + GLA_KEY_WIDTH
Z_V = Z_K + GLA_KEY_WIDTH
Z_G = Z_V + GLA_WIDTH
Z_P = Z_G + GLA_WIDTH
Z_GATE = Z_P + POOL_WIDTH
Z_GATE_PAD = 128
Z_WIDTH = Z_GATE + Z_GATE_PAD
W_AF = 2 * GLA_KEY_WIDTH + 2 * GLA_WIDTH
W_P = W_AF + 2 * GLA_GATE_RANK

P1_BLK = 256
POOL_BLK = 256
GU_SUB = 256
LN_SUB = 128
CHUNK_SHIFT = GLA_CHUNK.bit_length() - 1
GRID_SHIFT = GRID_W.bit_length() - 1
V7X_VMEM_LIMIT_BYTES = 58 * 2 ** 20


def _log_sigmoid(x):
    return jnp.minimum(x, 0.0) - jnp.log1p(jnp.exp(-jnp.abs(x)))


def _dot(a, b):
    return jnp.dot(a, b, preferred_element_type=F32)


def _dot_nt(a, b):
    return lax.dot_general(a, b, (((1,), (1,)), ((), ())), preferred_element_type=F32)


def _band_sum(band, x):
    hi = x.astype(BF16)
    r1 = x - hi.astype(F32)
    mid = r1.astype(BF16)
    lo = (r1 - mid.astype(F32)).astype(BF16)
    return _dot(band, hi) + _dot(band, mid) + _dot(band, lo)


def _cparams(semantics):
    return pltpu.CompilerParams(dimension_semantics=semantics, vmem_limit_bytes=V7X_VMEM_LIMIT_BYTES)


def _layer_norm(y, g, b):
    mu = jnp.mean(y, axis=-1, keepdims=True)
    yc = y - mu
    var = jnp.mean(yc * yc, axis=-1, keepdims=True)
    return yc * lax.rsqrt(var + LN_EPS) * g + b


def _mod_spec(layer, part, rows_per_mod, tm):
    if rows_per_mod is None:
        return pl.BlockSpec((None, None, 1, D_MODEL), lambda i, *_: (layer, 0, 0, part))
    return pl.BlockSpec((None, None, 1, D_MODEL),
                        lambda i, *_: (layer, 1 + (i * tm) // rows_per_mod, 0, part))


def _layer_vec_spec(layer, width):
    return pl.BlockSpec((None, 1, width), lambda *_: (layer, 0, 0))


def _ada_kernel(c_ref, w_ref, b_ref, o_ref):
    c = c_ref[...]
    s = jax.nn.silu(c).astype(BF16)
    o_ref[...] = _dot(s, w_ref[...].astype(BF16)) + b_ref[...]


def _ada_call(c_all, w_ada, b_ada):
    rows = c_all.shape[0]
    tn = 1024
    nj = (6 * D_MODEL) // tn
    return pl.pallas_call(
        _ada_kernel,
        grid=(DEPTH, nj),
        in_specs=[
            pl.BlockSpec((rows, D_MODEL), lambda l, j: (0, 0)),
            pl.BlockSpec((None, D_MODEL, tn), lambda l, j: (l, 0, j)),
            pl.BlockSpec((None, 1, tn), lambda l, j: (l, 0, j)),
        ],
        out_specs=pl.BlockSpec((None, rows, tn), lambda l, j: (l, 0, j)),
        out_shape=jax.ShapeDtypeStruct((DEPTH, rows, 6 * D_MODEL), F32),
        compiler_params=_cparams(("arbitrary", "arbitrary")),
        name="ada_mod",
    )(c_all, w_ada, b_ada.reshape(DEPTH, 1, 6 * D_MODEL))


def _ln_in_kernel(x_ref, g_ref, b_ref, sh_ref, sc_ref, x0_ref, h_ref):
    y = _layer_norm(x_ref[...], g_ref[...], b_ref[...])
    x0_ref[...] = y
    h_ref[...] = (y * (1.0 + sc_ref[...]) + sh_ref[...]).astype(BF16)


def _ln_in_call(x, g, b, mod4, rows_per_mod):
    m = x.shape[0]
    tm = min(512, m)
    row = pl.BlockSpec((tm, D_MODEL), lambda i: (i, 0))
    vec = pl.BlockSpec((1, D_MODEL), lambda i: (0, 0))
    return pl.pallas_call(
        _ln_in_kernel,
        grid=(m // tm,),
        in_specs=[row, vec, vec, _mod_spec(0, 0, rows_per_mod, tm), _mod_spec(0, 1, rows_per_mod, tm)],
        out_specs=[row, row],
        out_shape=[jax.ShapeDtypeStruct((m, D_MODEL), F32), jax.ShapeDtypeStruct((m, D_MODEL), BF16)],
        compiler_params=_cparams(("arbitrary",)),
        name="ln_in",
    )(x, g.reshape(1, D_MODEL), b.reshape(1, D_MODEL), mod4, mod4)


def _mm_kernel(h_ref, w_ref, o_ref):
    o_ref[...] = _dot(h_ref[...], w_ref[...])


def _mm_in_call(h, w, layer):
    m = h.shape[0]
    tm = min(512, m)
    return pl.pallas_call(
        _mm_kernel,
        grid=(m // tm,),
        in_specs=[
            pl.BlockSpec((tm, D_MODEL), lambda i: (i, 0)),
            pl.BlockSpec((None, D_MODEL, Z_WIDTH), lambda i: (layer, 0, 0), pipeline_mode=pl.Buffered(1)),
        ],
        out_specs=pl.BlockSpec((tm, Z_WIDTH), lambda i: (i, 0)),
        out_shape=jax.ShapeDtypeStruct((m, Z_WIDTH), F32),
        compiler_params=_cparams(("arbitrary",)),
        name="mm_in",
    )(h, w)


def _gla_kernel(*refs, seq_len, has_s0, emit_state, alias_state):
    (q_ref, k_ref, v_ref, g_ref, zg_ref, wg_ref, bg_ref, gn_ref, tl_ref) = refs[:9]
    pos = 9
    if has_s0:
        s0f_ref, s0b_ref = refs[pos:pos + 2]
        pos += 2
    if alias_state:
        pos += 1
    mo_ref = refs[pos]
    pos += 1
    if emit_state:
        st_ref = refs[pos]
        pos += 1
    qef, qeb, o_ref, u_ref, dec_ref = refs[pos:]

    n_blk = seq_len // P1_BLK
    n_chunks = seq_len // GLA_CHUNK
    c = GLA_CHUNK
    dk = GLA_DK

    brow = lax.broadcasted_iota(jnp.int32, (P1_BLK, P1_BLK), 0)
    bcol = lax.broadcasted_iota(jnp.int32, (P1_BLK, P1_BLK), 1)
    same_chunk = (brow >> CHUNK_SHIFT) == (bcol >> CHUNK_SHIFT)
    mask_f = jnp.logical_and(same_chunk, bcol <= brow)
    mask_b = jnp.logical_and(same_chunk, bcol >= brow)
    row_chunk = lax.broadcasted_iota(jnp.int32, (P1_BLK, dk), 0) >> CHUNK_SHIFT

    def phase_a(i, carry):
        r = pl.multiple_of(i * P1_BLK, P1_BLK)
        rows = pl.ds(r, P1_BLK)
        zg = zg_ref[rows, :].astype(BF16)
        la = _log_sigmoid(_dot(zg, wg_ref[...]) + bg_ref[...]) * (1.0 / GLA_GATE_TEMP)
        pre = _band_sum(tl_ref[...], la)
        n_sub = P1_BLK // c
        tot = jnp.concatenate(
            [jnp.broadcast_to(pre[(s + 1) * c - 1:(s + 1) * c, :], (c, 2 * dk)) for s in range(n_sub)], axis=0)
        b_f = pre[:, :dk]
        tot_f = tot[:, :dk]
        tot_b = tot[:, dk:]
        b_b = tot_b - pre[:, dk:] + la[:, dk:]
        q = q_ref[rows, :] * Q_SCALE
        k = k_ref[rows, :]
        qe_f = (q * jnp.exp(b_f)).astype(BF16)
        qe_b = (q * jnp.exp(b_b)).astype(BF16)
        qef[rows, :] = qe_f
        qeb[rows, :] = qe_b
        ke_f = (k * jnp.exp(-b_f)).astype(BF16)
        ke_b = (k * jnp.exp(-b_b)).astype(BF16)
        a = (jnp.where(mask_f, _dot_nt(qe_f, ke_f), 0.0) + jnp.where(mask_b, _dot_nt(qe_b, ke_b), 0.0))
        v = v_ref[rows, :]
        o_ref[rows, :] = _dot(a.astype(BF16), v.astype(BF16))
        kd_f = k * jnp.exp(tot_f - b_f)
        kd_b = k * jnp.exp(tot_b - b_b)
        kd_cols = [jnp.where(row_chunk == s, kd, 0.0).astype(BF16) for kd in (kd_f, kd_b) for s in range(n_sub)]
        u = _dot(v.T.astype(BF16), jnp.concatenate(kd_cols, axis=-1))
        for s in range(n_sub):
            ch = n_sub * i + s
            u_ref[ch] = u[:, s * dk:(s + 1) * dk]
            u_ref[n_chunks + ch] = u[:, (n_sub + s) * dk:(n_sub + s + 1) * dk]
            dec = jnp.exp(tot[s * c:s * c + 8, :])
            dec_ref[pl.ds(pl.multiple_of(8 * ch, 8), 8), :] = dec[:, :dk]
            dec_ref[pl.ds(pl.multiple_of(8 * (n_chunks + ch), 8), 8), :] = dec[:, dk:]
        return carry

    lax.fori_loop(0, n_blk, phase_a, 0, unroll=2 if n_blk % 2 == 0 else 1)

    def fwd_chunk(ch, st):
        rows = pl.ds(pl.multiple_of(ch * c, c), c)
        o_ref[rows, :] += _dot_nt(qef[rows, :], st.astype(BF16))
        return st * dec_ref[pl.ds(pl.multiple_of(8 * ch, 8), 1), :] + u_ref[ch]

    def bwd_chunk(j, st):
        ch = n_chunks - 1 - j
        rows = pl.ds(pl.multiple_of(ch * c, c), c)
        o = o_ref[rows, :] + _dot_nt(qeb[rows, :], st.astype(BF16))
        ms = jnp.mean(o * o, axis=-1, keepdims=True)
        y = o * lax.rsqrt(ms + LN_EPS) * gn_ref[...]
        mo_ref[rows, :] = (y * jax.nn.silu(g_ref[rows, :])).astype(BF16)
        return st * dec_ref[pl.ds(pl.multiple_of(8 * (n_chunks + ch), 8), 1), :] + u_ref[n_chunks + ch]

    if has_s0:
        st0_f = s0f_ref[...].T
        st0_b = s0b_ref[...].T
    else:
        st0_f = jnp.zeros((GLA_DV, dk), F32)
        st0_b = st0_f
    unroll = True if n_chunks <= 4 else 4
    st_f = lax.fori_loop(0, n_chunks, fwd_chunk, st0_f, unroll=unroll)
    st_b = lax.fori_loop(0, n_chunks, bwd_chunk, st0_b, unroll=unroll)
    if emit_state:
        st_ref[0] = st_f.T
        st_ref[1] = st_b.T


def _chunk_prefix_matrix():
    i = np.arange(P1_BLK)
    same = (i[:, None] >> CHUNK_SHIFT) == (i[None, :] >> CHUNK_SHIFT)
    return jnp.asarray(np.logical_and(same, i[None, :] <= i[:, None]).astype(np.float32), BF16)


def _gla_call(z, wgate, bgate, gnorm, layer, nseq, seq_len, s0=None, emit_state=False, state_buf=None):
    m = nseq * seq_len
    tl = _chunk_prefix_matrix()
    has_s0 = s0 is not None
    alias_state = state_buf is not None
    kw, vw = GLA_DK, GLA_DV
    in_specs = [
        pl.BlockSpec((seq_len, kw), lambda s, h: (s, Z_Q // kw + h)),
        pl.BlockSpec((seq_len, kw), lambda s, h: (s, Z_K // kw + h)),
        pl.BlockSpec((seq_len, vw), lambda s, h: (s, Z_V // vw + h)),
        pl.BlockSpec((seq_len, vw), lambda s, h: (s, Z_G // vw + h)),
        pl.BlockSpec((seq_len, Z_GATE_PAD), lambda s, h: (s, Z_GATE // Z_GATE_PAD)),
        pl.BlockSpec((None, Z_GATE_PAD, 2 * kw), lambda s, h: (layer, 0, h)),
        pl.BlockSpec((None, 1, 2 * kw), lambda s, h: (layer, 0, h)),
        pl.BlockSpec((None, 1, vw), lambda s, h: (layer, 0, h)),
        pl.BlockSpec((P1_BLK, P1_BLK), lambda s, h: (0, 0)),
    ]
    args = [z, z, z, z, z, wgate, bgate, gnorm, tl]
    if has_s0:
        in_specs += [
            pl.BlockSpec((None, None, None, None, kw, vw), lambda s, h: (layer, 0, s, h, 0, 0)),
            pl.BlockSpec((None, None, None, None, kw, vw), lambda s, h: (layer, 1, s, h, 0, 0)),
        ]
        args += [s0, s0]
    aliases = {}
    if alias_state:
        aliases = {len(args): 1}
        in_specs.append(pl.BlockSpec(memory_space=pl.ANY))
        args.append(state_buf)
    out_specs = [pl.BlockSpec((seq_len, vw), lambda s, h: (s, h))]
    out_shape = [jax.ShapeDtypeStruct((m, GLA_WIDTH), BF16)]
    if emit_state:
        out_specs.append(pl.BlockSpec((None, 2, None, None, kw, vw), lambda s, h: (layer, 0, s, h, 0, 0)))
        out_shape.append(jax.ShapeDtypeStruct((DEPTH, 2, nseq, GLA_HEADS, kw, vw), F32))
    n_chunks = seq_len // GLA_CHUNK
    scratch = [
        pltpu.VMEM((seq_len, kw), BF16),
        pltpu.VMEM((seq_len, kw), BF16),
        pltpu.VMEM((seq_len, vw), F32),
        pltpu.VMEM((2 * n_chunks, vw, kw), F32),
        pltpu.VMEM((2 * n_chunks * 8, kw), F32),
    ]
    return pl.pallas_call(
        functools.partial(_gla_kernel, seq_len=seq_len, has_s0=has_s0, emit_state=emit_state,
                          alias_state=alias_state),
        grid=(nseq, GLA_HEADS),
        in_specs=in_specs,
        out_specs=out_specs,
        out_shape=out_shape,
        scratch_shapes=scratch,
        input_output_aliases=aliases,
        compiler_params=_cparams(("arbitrary", "arbitrary")),
        name="gla_state" if emit_state else "gla",
    )(*args)


def _window_count(idx, n, w):
    lo = jnp.maximum(idx - w // 2, 0)
    hi = jnp.minimum(idx - w // 2 + w, n)
    return hi - lo


def _pool_project(pooled, gi, w_ref, sc_ref, o_ref, rows):
    cols = slice(gi * POOL_GROUP_DIM, (gi + 1) * POOL_GROUP_DIM)
    y = _dot(pooled.astype(BF16), w_ref[gi]) * sc_ref[:, cols]
    o_ref[rows, cols] = y.astype(BF16)


def _shifted(x, pos, n, d):
    rolled = pltpu.roll(x, (-d) % x.shape[0], 0)
    return jnp.where(jnp.logical_and(pos + d >= 0, pos + d < n), rolled, 0.0)


def _window_sum(x, pos, n, w):
    fwd = bwd = x
    s = 1
    while s < w // 2:
        fwd = fwd + _shifted(fwd, pos, n, s)
        bwd = bwd + _shifted(bwd, pos, n, -s)
        s *= 2
    return fwd + _shifted(bwd, pos, n, -1)


def _pool1d_kernel(p_ref, w_ref, sc_ref, o_ref, *, seq_len):
    t = lax.broadcasted_iota(jnp.int32, (seq_len, 1), 0)
    for gi, w in enumerate(POOL_WINDOWS):
        x = p_ref[:, gi * POOL_GROUP_DIM:(gi + 1) * POOL_GROUP_DIM]
        cnt = _window_count(t, seq_len, w).astype(F32)
        _pool_project(_window_sum(x, t, seq_len, w) / cnt - x, gi, w_ref, sc_ref, o_ref, slice(None))


def _pool2d_kernel(p_ref, w_ref, sc_ref, o_ref, cs_ref, *, seq_len):
    n_rows = seq_len // GRID_W
    pad = (max(POOL_WINDOWS) // 2) * GRID_W
    n_blk = seq_len // POOL_BLK
    zeros = jnp.zeros((pad, POOL_GROUP_DIM), F32)
    cs_ref[pl.ds(0, pad), :] = zeros
    cs_ref[pl.ds(pad + seq_len, pad), :] = zeros
    local = lax.broadcasted_iota(jnp.int32, (POOL_BLK, 1), 0)
    col = local & (GRID_W - 1)
    for gi, w in enumerate(POOL_WINDOWS):
        cols = slice(gi * POOL_GROUP_DIM, (gi + 1) * POOL_GROUP_DIM)

        def col_sums(i, carry, w=w, cols=cols):
            r = pl.multiple_of(i * POOL_BLK, POOL_BLK)
            acc = _window_sum(p_ref[pl.ds(r, POOL_BLK), cols], col, GRID_W, w)
            cs_ref[pl.ds(pl.multiple_of(r + pad, GRID_W), POOL_BLK), :] = acc
            return carry

        lax.fori_loop(0, n_blk, col_sums, 0)

        def row_sums(i, carry, w=w, cols=cols, gi=gi):
            r = pl.multiple_of(i * POOL_BLK, POOL_BLK)
            acc = jnp.zeros((POOL_BLK, POOL_GROUP_DIM), F32)
            for j in range(w):
                off = pad + (j - w // 2) * GRID_W
                acc = acc + cs_ref[pl.ds(pl.multiple_of(r + off, GRID_W), POOL_BLK), :]
            grow = (r + local) >> GRID_SHIFT
            cnt = (_window_count(grow, n_rows, w) * _window_count(col, GRID_W, w)).astype(F32)
            x = p_ref[pl.ds(r, POOL_BLK), cols]
            _pool_project(acc / cnt - x, gi, w_ref, sc_ref, o_ref, pl.ds(r, POOL_BLK))
            return carry

        lax.fori_loop(0, n_blk, row_sums, 0, unroll=2)


def _pool_call(z, w_pool, pool_scale, layer, nseq, seq_len, on_grid):
    m = nseq * seq_len
    if on_grid:
        body = functools.partial(_pool2d_kernel, seq_len=seq_len)
        pad = (max(POOL_WINDOWS) // 2) * GRID_W
        scratch = [pltpu.VMEM((seq_len + 2 * pad, POOL_GROUP_DIM), F32)]
    else:
        body = functools.partial(_pool1d_kernel, seq_len=seq_len)
        scratch = []
    n_groups = len(POOL_WINDOWS)
    return pl.pallas_call(
        body,
        grid=(nseq,),
        in_specs=[
            pl.BlockSpec((seq_len, POOL_WIDTH), lambda s: (s, Z_P // POOL_WIDTH)),
            pl.BlockSpec((None, n_groups, POOL_GROUP_DIM, POOL_GROUP_DIM), lambda s: (layer, 0, 0, 0)),
            _layer_vec_spec(layer, POOL_WIDTH),
        ],
        out_specs=pl.BlockSpec((seq_len, POOL_WIDTH), lambda s: (s, 0)),
        out_shape=jax.ShapeDtypeStruct((m, POOL_WIDTH), BF16),
        scratch_shapes=scratch,
        compiler_params=_cparams(("arbitrary",)),
        name="pool2d" if on_grid else "pool1d",
    )(z, w_pool, pool_scale)


def _mm_out_ln_kernel(mo_ref, mp_ref, w_ref, x_ref, g1_ref, sh2_ref, sc2_ref, lg_ref, lb_ref,
                      x1_ref, h2_ref):
    for s in range(x_ref.shape[0] // LN_SUB):
        rows = pl.ds(s * LN_SUB, LN_SUB)
        m = _dot(mo_ref[rows, :], w_ref[:GLA_WIDTH, :]) + _dot(mp_ref[rows, :], w_ref[GLA_WIDTH:, :])
        x1 = _layer_norm(ALPHA * x_ref[rows, :] + g1_ref[...] * m, lg_ref[...], lb_ref[...])
        x1_ref[rows, :] = x1
        h2_ref[rows, :] = (x1 * (1.0 + sc2_ref[...]) + sh2_ref[...]).astype(BF16)


def _mm_out_ln_call(mo, mp, w_out, x, mod4, rows_per_mod, ln_g, ln_b, layer):
    m = x.shape[0]
    tm = min(512, m)
    row = pl.BlockSpec((tm, D_MODEL), lambda i: (i, 0))
    half = pl.BlockSpec((tm, GLA_WIDTH), lambda i: (i, 0))
    w_spec = pl.BlockSpec((None, D_MODEL, D_MODEL), lambda i: (layer, 0, 0), pipeline_mode=pl.Buffered(1))
    mod = lambda part: _mod_spec(layer, part, rows_per_mod, tm)
    return pl.pallas_call(
        _mm_out_ln_kernel,
        grid=(m // tm,),
        in_specs=[half, half, w_spec, row, mod(2), mod(3), mod(4),
                  _layer_vec_spec(layer, D_MODEL), _layer_vec_spec(layer, D_MODEL)],
        out_specs=[row, row],
        out_shape=[jax.ShapeDtypeStruct((m, D_MODEL), F32), jax.ShapeDtypeStruct((m, D_MODEL), BF16)],
        compiler_params=_cparams(("arbitrary",)),
        name="mm_out_ln",
    )(mo, mp, w_out, x, mod4, mod4, mod4, ln_g, ln_b)


def _mm_gu_kernel(h_ref, wg_ref, wu_ref, o_ref):
    h = h_ref[...]
    for c in range(o_ref.shape[1] // GU_SUB):
        cols = pl.ds(c * GU_SUB, GU_SUB)
        gate = _dot(h, wg_ref[:, cols].astype(BF16))
        up = _dot(h, wu_ref[:, cols].astype(BF16))
        o_ref[:, cols] = (jax.nn.silu(gate) * up).astype(BF16)


def _mm_gu_call(h, w_gu, layer):
    m = h.shape[0]
    tm = min(1024, m)
    tn = 512
    nj = D_FF // tn
    return pl.pallas_call(
        _mm_gu_kernel,
        grid=(nj, m // tm),
        in_specs=[
            pl.BlockSpec((tm, D_MODEL), lambda j, i: (i, 0)),
            pl.BlockSpec((None, D_MODEL, tn), lambda j, i: (layer, 0, j)),
            pl.BlockSpec((None, D_MODEL, tn), lambda j, i: (layer, 0, nj + j)),
        ],
        out_specs=pl.BlockSpec((tm, tn), lambda j, i: (i, j)),
        out_shape=jax.ShapeDtypeStruct((m, D_FF), BF16),
        compiler_params=_cparams(("arbitrary", "arbitrary")),
        name="mm_gu",
    )(h, w_gu, w_gu)


def _mm_down_ln_kernel(*refs, emit_h):
    a_ref, w_ref, x1_ref, g2_ref = refs[:4]
    pos = 4
    if emit_h:
        shn_ref, scn_ref = refs[pos:pos + 2]
        pos += 2
    lg_ref, lb_ref, x_ref = refs[pos:pos + 3]
    pos += 3
    if emit_h:
        h_ref = refs[pos]
    for s in range(x_ref.shape[0] // LN_SUB):
        rows = pl.ds(s * LN_SUB, LN_SUB)
        f = _dot(a_ref[rows, :], w_ref[...])
        x = _layer_norm(ALPHA * x1_ref[rows, :] + g2_ref[...] * f, lg_ref[...], lb_ref[...])
        x_ref[rows, :] = x
        if emit_h:
            h_ref[rows, :] = (x * (1.0 + scn_ref[...]) + shn_ref[...]).astype(BF16)


def _mm_down_ln_call(act, w_down, x1, mod4, rows_per_mod, ln_g, ln_b, layer):
    m = x1.shape[0]
    tm = min(512, m)
    emit_h = layer + 1 < DEPTH
    row = pl.BlockSpec((tm, D_MODEL), lambda i: (i, 0))
    in_specs = [
        pl.BlockSpec((tm, D_FF), lambda i: (i, 0)),
        pl.BlockSpec((None, D_FF, D_MODEL), lambda i: (layer, 0, 0), pipeline_mode=pl.Buffered(1)),
        row,
        _mod_spec(layer, 5, rows_per_mod, tm),
    ]
    args = [act, w_down, x1, mod4]
    if emit_h:
        in_specs += [_mod_spec(layer + 1, 0, rows_per_mod, tm), _mod_spec(layer + 1, 1, rows_per_mod, tm)]
        args += [mod4, mod4]
    in_specs += [_layer_vec_spec(layer, D_MODEL), _layer_vec_spec(layer, D_MODEL)]
    args += [ln_g, ln_b]
    out_specs = [row]
    out_shape = [jax.ShapeDtypeStruct((m, D_MODEL), F32)]
    if emit_h:
        out_specs.append(row)
        out_shape.append(jax.ShapeDtypeStruct((m, D_MODEL), BF16))
    outs = pl.pallas_call(
        functools.partial(_mm_down_ln_kernel, emit_h=emit_h),
        grid=(m // tm,),
        in_specs=in_specs,
        out_specs=out_specs,
        out_shape=out_shape,
        compiler_params=_cparams(("arbitrary",)),
        name="mm_down_ln",
    )(*args)
    return (outs[0], outs[1]) if emit_h else (outs[0], None)


def _prep_weights(w_in, w_a2, b_a, w_pool, w_out, w_down):
    depth = w_in.shape[0]
    gate_pad = jnp.zeros((depth, D_MODEL, Z_GATE_PAD - 2 * GLA_GATE_RANK), w_in.dtype)
    w_in_r = jnp.concatenate(
        [w_in[:, :, :W_AF], w_in[:, :, W_P:], w_in[:, :, W_AF:W_P], gate_pad], axis=-1).astype(BF16)
    wgate = jnp.zeros((depth, Z_GATE_PAD, GLA_HEADS, 2, GLA_DK), F32)
    per_head = lambda t: t.reshape(depth, GLA_GATE_RANK, GLA_HEADS, GLA_DK)
    wgate = wgate.at[:, :GLA_GATE_RANK, :, 0, :].set(per_head(w_a2[:, 0]))
    wgate = wgate.at[:, GLA_GATE_RANK:2 * GLA_GATE_RANK, :, 1, :].set(per_head(w_a2[:, 1]))
    wgate = wgate.reshape(depth, Z_GATE_PAD, 2 * GLA_KEY_WIDTH)
    bgate = b_a.reshape(depth, 2, GLA_HEADS, GLA_DK).transpose(0, 2, 1, 3).reshape(depth, 1, 2 * GLA_KEY_WIDTH)
    return (w_in_r, wgate.astype(BF16), bgate, w_pool.astype(BF16), w_out.astype(BF16),
            w_down.astype(BF16))


def kernel(x_prompt, x_sample, state_gla, c, c_ctx, ln_in_g, ln_in_b, w_ada, b_ada, w_in, w_a2, b_a,
           gla_norm_g, w_pool, pool_scale, w_out, ln1_g, ln1_b, w_gu, w_down, ln2_g, ln2_b):
    nb_p, len_p, _ = x_prompt.shape
    nb_s, len_s, _ = x_sample.shape
    w_in_r, wgate, bgate, w_pool_b, w_out_b, w_down_b = _prep_weights(
        w_in, w_a2, b_a, w_pool, w_out, w_down)
    gnorm = gla_norm_g.reshape(DEPTH, 1, GLA_WIDTH)
    pscale = pool_scale.reshape(DEPTH, 1, POOL_WIDTH)
    ln1 = (ln1_g.reshape(DEPTH, 1, D_MODEL), ln1_b.reshape(DEPTH, 1, D_MODEL))
    ln2 = (ln2_g.reshape(DEPTH, 1, D_MODEL), ln2_b.reshape(DEPTH, 1, D_MODEL))

    n_mod = -(-(1 + nb_s) // 8) * 8
    c_all = jnp.concatenate(
        [c_ctx[None, :], c, jnp.zeros((n_mod - 1 - nb_s, D_MODEL), c.dtype)], axis=0)
    mod4 = _ada_call(c_all, w_ada, b_ada).reshape(DEPTH, n_mod, 1, 6 * D_MODEL)

    streams = [
        (x_prompt.reshape(nb_p * len_p, D_MODEL), None, nb_p, len_p, False),
        (x_sample.reshape(nb_s * len_s, D_MODEL), len_s, nb_s, len_s, True),
    ]
    results = []
    state_buf = None
    for x_in, rows_per_mod, nseq, seq_len, on_grid in streams:
        x, h = _ln_in_call(x_in, ln_in_g, ln_in_b, mod4, rows_per_mod)
        for l in range(DEPTH):
            z = _mm_in_call(h, w_in_r, l)
            if on_grid:
                (mo,) = _gla_call(z, wgate, bgate, gnorm, l, nseq, seq_len, s0=state_gla)
            else:
                mo, state_buf = _gla_call(z, wgate, bgate, gnorm, l, nseq, seq_len,
                                          emit_state=True, state_buf=state_buf)
            mp = _pool_call(z, w_pool_b, pscale, l, nseq, seq_len, on_grid)
            x1, h2 = _mm_out_ln_call(mo, mp, w_out_b, x, mod4, rows_per_mod, ln1[0], ln1[1], l)
            act = _mm_gu_call(h2, w_gu, l)
            x, h = _mm_down_ln_call(act, w_down_b, x1, mod4, rows_per_mod, ln2[0], ln2[1], l)
        results.append(x)
    y_prompt = results[0].reshape(nb_p, len_p, D_MODEL)
    y_sample = results[1].reshape(nb_s, len_s, D_MODEL)
    return (y_prompt, y_sample, state_buf.astype(x_prompt.dtype))
```

```python
import functools

import numpy as np
import jax
import jax.numpy as jnp
from jax import lax
from jax.experimental import pallas as pl
from jax.experimental.pallas import tpu as pltpu

F32 = jnp.float32
BF16 = jnp.bfloat16

D_MODEL = 2048
DEPTH = 4
GLA_HEADS = 4
GLA_DV = 256
GLA_DK = 128
GLA_KEY_WIDTH = GLA_HEADS * GLA_DK
GLA_WIDTH = GLA_HEADS * GLA_DV
POOL_WIDTH = 1024
GLA_GATE_RANK = 16
GLA_GATE_TEMP = 16.0
GLA_CHUNK = 64
POOL_WINDOWS = (2, 4, 8, 16)
POOL_GROUP_DIM = 256
GRID_W = 64
D_FF = 5632
ALPHA = (2 * DEPTH) ** 0.25
LN_EPS = 1e-5
Q_SCALE = GLA_DK ** -0.5

Z_Q = 0
Z_K = Z_Q + GLA_KEY_WIDTH
Z_V = Z_K + GLA_KEY_WIDTH
Z_G = Z_V + GLA_WIDTH
Z_P = Z_G + GLA_WIDTH
Z_GATE = Z_P + POOL_WIDTH
Z_GATE_PAD = 128
Z_WIDTH = Z_GATE + Z_GATE_PAD
W_AF = 2 * GLA_KEY_WIDTH + 2 * GLA_WIDTH
W_P = W_AF + 2 * GLA_GATE_RANK

P1_BLK = 256
POOL_BLK = 256
GU_SUB = 256
LN_SUB = 128
CHUNK_SHIFT = GLA_CHUNK.bit_length() - 1
GRID_SHIFT = GRID_W.bit_length() - 1
V7X_VMEM_LIMIT_BYTES = 58 * 2 ** 20


def _log_sigmoid(x):
    return jnp.minimum(x, 0.0) - jnp.log1p(jnp.exp(-jnp.abs(x)))


def _dot(a, b):
    return jnp.dot(a, b, preferred_element_type=F32)


def _dot_nt(a, b):
    return lax.dot_general(a, b, (((1,), (1,)), ((), ())), preferred_element_type=F32)


def _band_sum(band, x):
    hi = x.astype(BF16)
    r1 = x - hi.astype(F32)
    mid = r1.astype(BF16)
    lo = (r1 - mid.astype(F32)).astype(BF16)
    return _dot(band, hi) + _dot(band, mid) + _dot(band, lo)


def _cparams(semantics):
    return pltpu.CompilerParams(dimension_semantics=semantics, vmem_limit_bytes=V7X_VMEM_LIMIT_BYTES)


def _layer_norm(y, g, b):
    mu = jnp.mean(y, axis=-1, keepdims=True)
    yc = y - mu
    var = jnp.mean(yc * yc, axis=-1, keepdims=True)
    return yc * lax.rsqrt(var + LN_EPS) * g + b


def _mod_spec(layer, part, rows_per_mod, tm):
    if rows_per_mod is None:
        return pl.BlockSpec((None, None, 1, D_MODEL), lambda i, *_: (layer, 0, 0, part))
    return pl.BlockSpec((None, None, 1, D_MODEL),
                        lambda i, *_: (layer, 1 + (i * tm) // rows_per_mod, 0, part))


def _layer_vec_spec(layer, width):
    return pl.BlockSpec((None, 1, width), lambda *_: (layer, 0, 0))


def _ada_kernel(c_ref, w_ref, b_ref, o_ref):
    c = c_ref[...]
    s = jax.nn.silu(c).astype(BF16)
    o_ref[...] = _dot(s, w_ref[...].astype(BF16)) + b_ref[...]


def _ada_call(c_all, w_ada, b_ada):
    rows = c_all.shape[0]
    tn = 1024
    nj = (6 * D_MODEL) // tn
    return pl.pallas_call(
        _ada_kernel,
        grid=(DEPTH, nj),
        in_specs=[
            pl.BlockSpec((rows, D_MODEL), lambda l, j: (0, 0)),
            pl.BlockSpec((None, D_MODEL, tn), lambda l, j: (l, 0, j)),
            pl.BlockSpec((None, 1, tn), lambda l, j: (l, 0, j)),
        ],
        out_specs=pl.BlockSpec((None, rows, tn), lambda l, j: (l, 0, j)),
        out_shape=jax.ShapeDtypeStruct((DEPTH, rows, 6 * D_MODEL), F32),
        compiler_params=_cparams(("arbitrary", "arbitrary")),
        name="ada_mod",
    )(c_all, w_ada, b_ada.reshape(DEPTH, 1, 6 * D_MODEL))


def _ln_mm_in_kernel(y_ref, lg_ref, lb_ref, sh_ref, sc_ref, w_ref, z_ref, x_ref):
    for s in range(y_ref.shape[0] // LN_SUB):
        rows = pl.ds(s * LN_SUB, LN_SUB)
        x = _layer_norm(y_ref[rows, :], lg_ref[...], lb_ref[...])
        x_ref[rows, :] = x
        h = (x * (1.0 + sc_ref[...]) + sh_ref[...]).astype(BF16)
        z_ref[rows, :] = _dot(h, w_ref[...])


def _ln_mm_in_call(y, ln_g, ln_b, ln_layer, mod4, rows_per_mod, w, layer):
    m = y.shape[0]
    tm = min(512, m)
    row = pl.BlockSpec((tm, D_MODEL), lambda i: (i, 0))
    return pl.pallas_call(
        _ln_mm_in_kernel,
        grid=(m // tm,),
        in_specs=[
            row, _layer_vec_spec(ln_layer, D_MODEL), _layer_vec_spec(ln_layer, D_MODEL),
            _mod_spec(layer, 0, rows_per_mod, tm), _mod_spec(layer, 1, rows_per_mod, tm),
            pl.BlockSpec((None, D_MODEL, Z_WIDTH), lambda i: (layer, 0, 0), pipeline_mode=pl.Buffered(1)),
        ],
        out_specs=[pl.BlockSpec((tm, Z_WIDTH), lambda i: (i, 0)), row],
        out_shape=[jax.ShapeDtypeStruct((m, Z_WIDTH), F32), jax.ShapeDtypeStruct((m, D_MODEL), F32)],
        compiler_params=_cparams(("arbitrary",)),
        name="ln_mm_in",
    )(y, ln_g, ln_b, mod4, mod4, w)


def _gla_kernel(*refs, seq_len, has_s0, emit_state):
    (q_ref, k_ref, v_ref, g_ref, zg_ref, wg_ref, bg_ref, gn_ref, tl_ref) = refs[:9]
    pos = 9
    if has_s0:
        s0f_ref, s0b_ref = refs[pos:pos + 2]
        pos += 2
    if emit_state:
        pos += 1
    mo_ref = refs[pos]
    pos += 1
    if emit_state:
        st_ref = refs[pos]
        pos += 1
    qef, qeb, o_ref, u_ref, dec_ref = refs[pos:]

    n_blk = seq_len // P1_BLK
    n_chunks = seq_len // GLA_CHUNK
    c = GLA_CHUNK
    dk = GLA_DK

    brow = lax.broadcasted_iota(jnp.int32, (P1_BLK, P1_BLK), 0)
    bcol = lax.broadcasted_iota(jnp.int32, (P1_BLK, P1_BLK), 1)
    same_chunk = (brow >> CHUNK_SHIFT) == (bcol >> CHUNK_SHIFT)
    mask_f = jnp.logical_and(same_chunk, bcol <= brow)
    mask_b = jnp.logical_and(same_chunk, bcol >= brow)
    row_chunk = lax.broadcasted_iota(jnp.int32, (P1_BLK, dk), 0) >> CHUNK_SHIFT

    def phase_a(i, carry):
        r = pl.multiple_of(i * P1_BLK, P1_BLK)
        rows = pl.ds(r, P1_BLK)
        zg = zg_ref[rows, :].astype(BF16)
        la = _log_sigmoid(_dot(zg, wg_ref[...]) + bg_ref[...]) * (1.0 / GLA_GATE_TEMP)
        pre = _band_sum(tl_ref[...], la)
        n_sub = P1_BLK // c
        tot = jnp.concatenate(
            [jnp.broadcast_to(pre[(s + 1) * c - 1:(s + 1) * c, :], (c, 2 * dk)) for s in range(n_sub)], axis=0)
        b_f = pre[:, :dk]
        tot_f = tot[:, :dk]
        tot_b = tot[:, dk:]
        b_b = tot_b - pre[:, dk:] + la[:, dk:]
        q = q_ref[rows, :] * Q_SCALE
        k = k_ref[rows, :]
        qe_f = (q * jnp.exp(b_f)).astype(BF16)
        qe_b = (q * jnp.exp(b_b)).astype(BF16)
        qef[rows, :] = qe_f
        qeb[rows, :] = qe_b
        ke_f = (k * jnp.exp(-b_f)).astype(BF16)
        ke_b = (k * jnp.exp(-b_b)).astype(BF16)
        a = (jnp.where(mask_f, _dot_nt(qe_f, ke_f), 0.0) + jnp.where(mask_b, _dot_nt(qe_b, ke_b), 0.0))
        v = v_ref[rows, :]
        o_ref[rows, :] = _dot(a.astype(BF16), v.astype(BF16))
        kd_f = k * jnp.exp(tot_f - b_f)
        kd_b = k * jnp.exp(tot_b - b_b)
        kd_cols = [jnp.where(row_chunk == s, kd, 0.0).astype(BF16) for kd in (kd_f, kd_b) for s in range(n_sub)]
        u = _dot(v.T.astype(BF16), jnp.concatenate(kd_cols, axis=-1))
        for s in range(n_sub):
            ch = n_sub * i + s
            u_ref[ch] = u[:, s * dk:(s + 1) * dk]
            u_ref[n_chunks + ch] = u[:, (n_sub + s) * dk:(n_sub + s + 1) * dk]
            dec = jnp.exp(tot[s * c:s * c + 8, :])
            dec_ref[pl.ds(pl.multiple_of(8 * ch, 8), 8), :] = dec[:, :dk]
            dec_ref[pl.ds(pl.multiple_of(8 * (n_chunks + ch), 8), 8), :] = dec[:, dk:]
        return carry

    lax.fori_loop(0, n_blk, phase_a, 0, unroll=2 if n_blk % 2 == 0 else 1)

    def fwd_chunk(ch, st):
        rows = pl.ds(pl.multiple_of(ch * c, c), c)
        o_ref[rows, :] += _dot_nt(qef[rows, :], st.astype(BF16))
        return st * dec_ref[pl.ds(pl.multiple_of(8 * ch, 8), 1), :] + u_ref[ch]

    def bwd_chunk(j, st):
        ch = n_chunks - 1 - j
        rows = pl.ds(pl.multiple_of(ch * c, c), c)
        o = o_ref[rows, :] + _dot_nt(qeb[rows, :], st.astype(BF16))
        ms = jnp.mean(o * o, axis=-1, keepdims=True)
        y = o * lax.rsqrt(ms + LN_EPS) * gn_ref[...]
        mo_ref[rows, :] = (y * jax.nn.silu(g_ref[rows, :])).astype(BF16)
        return st * dec_ref[pl.ds(pl.multiple_of(8 * (n_chunks + ch), 8), 1), :] + u_ref[n_chunks + ch]

    if has_s0:
        st0_f = s0f_ref[...].T
        st0_b = s0b_ref[...].T
    else:
        st0_f = jnp.zeros((GLA_DV, dk), F32)
        st0_b = st0_f
    unroll = True if n_chunks <= 4 else 4
    st_f = lax.fori_loop(0, n_chunks, fwd_chunk, st0_f, unroll=unroll)
    st_b = lax.fori_loop(0, n_chunks, bwd_chunk, st0_b, unroll=unroll)
    if emit_state:
        st_ref[0] = st_f.T
        st_ref[1] = st_b.T


def _chunk_prefix_matrix():
    i = np.arange(P1_BLK)
    same = (i[:, None] >> CHUNK_SHIFT) == (i[None, :] >> CHUNK_SHIFT)
    return jnp.asarray(np.logical_and(same, i[None, :] <= i[:, None]).astype(np.float32), BF16)


def _gla_call(z, wgate, bgate, gnorm, layer, nseq, seq_len, s0=None, state_buf=None):
    m = nseq * seq_len
    tl = _chunk_prefix_matrix()
    has_s0 = s0 is not None
    emit_state = state_buf is not None
    kw, vw = GLA_DK, GLA_DV
    in_specs = [
        pl.BlockSpec((seq_len, kw), lambda s, h: (s, Z_Q // kw + h)),
        pl.BlockSpec((seq_len, kw), lambda s, h: (s, Z_K // kw + h)),
        pl.BlockSpec((seq_len, vw), lambda s, h: (s, Z_V // vw + h)),
        pl.BlockSpec((seq_len, vw), lambda s, h: (s, Z_G // vw + h)),
        pl.BlockSpec((seq_len, Z_GATE_PAD), lambda s, h: (s, Z_GATE // Z_GATE_PAD)),
        pl.BlockSpec((None, Z_GATE_PAD, 2 * kw), lambda s, h: (layer, 0, h)),
        pl.BlockSpec((None, 1, 2 * kw), lambda s, h: (layer, 0, h)),
        pl.BlockSpec((None, 1, vw), lambda s, h: (layer, 0, h)),
        pl.BlockSpec((P1_BLK, P1_BLK), lambda s, h: (0, 0)),
    ]
    args = [z, z, z, z, z, wgate, bgate, gnorm, tl]
    if has_s0:
        in_specs += [
            pl.BlockSpec((None, None, None, None, kw, vw), lambda s, h: (layer, 0, s, h, 0, 0)),
            pl.BlockSpec((None, None, None, None, kw, vw), lambda s, h: (layer, 1, s, h, 0, 0)),
        ]
        args += [s0, s0]
    aliases = {}
    if emit_state:
        aliases = {len(args): 1}
        in_specs.append(pl.BlockSpec(memory_space=pl.ANY))
        args.append(state_buf)
    out_specs = [pl.BlockSpec((seq_len, vw), lambda s, h: (s, h))]
    out_shape = [jax.ShapeDtypeStruct((m, GLA_WIDTH), BF16)]
    if emit_state:
        out_specs.append(pl.BlockSpec((None, 2, None, None, kw, vw), lambda s, h: (layer, 0, s, h, 0, 0)))
        out_shape.append(jax.ShapeDtypeStruct((DEPTH, 2, nseq, GLA_HEADS, kw, vw), F32))
    n_chunks = seq_len // GLA_CHUNK
    scratch = [
        pltpu.VMEM((seq_len, kw), BF16),
        pltpu.VMEM((seq_len, kw), BF16),
        pltpu.VMEM((seq_len, vw), F32),
        pltpu.VMEM((2 * n_chunks, vw, kw), F32),
        pltpu.VMEM((2 * n_chunks * 8, kw), F32),
    ]
    return pl.pallas_call(
        functools.partial(_gla_kernel, seq_len=seq_len, has_s0=has_s0, emit_state=emit_state),
        grid=(nseq, GLA_HEADS),
        in_specs=in_specs,
        out_specs=out_specs,
        out_shape=out_shape,
        scratch_shapes=scratch,
        input_output_aliases=aliases,
        compiler_params=_cparams(("arbitrary", "arbitrary")),
        name="gla_state" if emit_state else "gla",
    )(*args)


def _window_count(idx, n, w):
    lo = jnp.maximum(idx - w // 2, 0)
    hi = jnp.minimum(idx - w // 2 + w, n)
    return hi - lo


def _pool_project(pooled, gi, w_ref, sc_ref, o_ref, rows):
    cols = slice(gi * POOL_GROUP_DIM, (gi + 1) * POOL_GROUP_DIM)
    y = _dot(pooled.astype(BF16), w_ref[gi]) * sc_ref[:, cols]
    o_ref[rows, cols] = y.astype(BF16)


def _shifted(x, pos, n, d):
    rolled = pltpu.roll(x, (-d) % x.shape[0], 0)
    return jnp.where(jnp.logical_and(pos + d >= 0, pos + d < n), rolled, 0.0)


def _window_sum(x, pos, n, w):
    fwd = bwd = x
    s = 1
    while s < w // 2:
        fwd = fwd + _shifted(fwd, pos, n, s)
        bwd = bwd + _shifted(bwd, pos, n, -s)
        s *= 2
    return fwd + _shifted(bwd, pos, n, -1)


def _pool1d_kernel(p_ref, w_ref, sc_ref, o_ref, *, seq_len):
    t = lax.broadcasted_iota(jnp.int32, (seq_len, 1), 0)
    for gi, w in enumerate(POOL_WINDOWS):
        x = p_ref[:, gi * POOL_GROUP_DIM:(gi + 1) * POOL_GROUP_DIM]
        cnt = _window_count(t, seq_len, w).astype(F32)
        _pool_project(_window_sum(x, t, seq_len, w) / cnt - x, gi, w_ref, sc_ref, o_ref, slice(None))


def _pool2d_kernel(p_ref, w_ref, sc_ref, o_ref, cs_ref, *, seq_len):
    n_rows = seq_len // GRID_W
    pad = (max(POOL_WINDOWS) // 2) * GRID_W
    n_blk = seq_len // POOL_BLK
    zeros = jnp.zeros((pad, POOL_GROUP_DIM), F32)
    cs_ref[pl.ds(0, pad), :] = zeros
    cs_ref[pl.ds(pad + seq_len, pad), :] = zeros
    local = lax.broadcasted_iota(jnp.int32, (POOL_BLK, 1), 0)
    col = local & (GRID_W - 1)
    for gi, w in enumerate(POOL_WINDOWS):
        cols = slice(gi * POOL_GROUP_DIM, (gi + 1) * POOL_GROUP_DIM)

        def col_sums(i, carry, w=w, cols=cols):
            r = pl.multiple_of(i * POOL_BLK, POOL_BLK)
            acc = _window_sum(p_ref[pl.ds(r, POOL_BLK), cols], col, GRID_W, w)
            cs_ref[pl.ds(pl.multiple_of(r + pad, GRID_W), POOL_BLK), :] = acc
            return carry

        lax.fori_loop(0, n_blk, col_sums, 0)

        def row_sums(i, carry, w=w, cols=cols, gi=gi):
            r = pl.multiple_of(i * POOL_BLK, POOL_BLK)
            acc = jnp.zeros((POOL_BLK, POOL_GROUP_DIM), F32)
            for j in range(w):
                off = pad + (j - w // 2) * GRID_W
                acc = acc + cs_ref[pl.ds(pl.multiple_of(r + off, GRID_W), POOL_BLK), :]
            grow = (r + local) >> GRID_SHIFT
            cnt = (_window_count(grow, n_rows, w) * _window_count(col, GRID_W, w)).astype(F32)
            x = p_ref[pl.ds(r, POOL_BLK), cols]
            _pool_project(acc / cnt - x, gi, w_ref, sc_ref, o_ref, pl.ds(r, POOL_BLK))
            return carry

        lax.fori_loop(0, n_blk, row_sums, 0, unroll=2)


def _pool_call(z, w_pool, pool_scale, layer, nseq, seq_len, on_grid):
    m = nseq * seq_len
    if on_grid:
        body = functools.partial(_pool2d_kernel, seq_len=seq_len)
        pad = (max(POOL_WINDOWS) // 2) * GRID_W
        scratch = [pltpu.VMEM((seq_len + 2 * pad, POOL_GROUP_DIM), F32)]
    else:
        body = functools.partial(_pool1d_kernel, seq_len=seq_len)
        scratch = []
    n_groups = len(POOL_WINDOWS)
    return pl.pallas_call(
        body,
        grid=(nseq,),
        in_specs=[
            pl.BlockSpec((seq_len, POOL_WIDTH), lambda s: (s, Z_P // POOL_WIDTH)),
            pl.BlockSpec((None, n_groups, POOL_GROUP_DIM, POOL_GROUP_DIM), lambda s: (layer, 0, 0, 0)),
            _layer_vec_spec(layer, POOL_WIDTH),
        ],
        out_specs=pl.BlockSpec((seq_len, POOL_WIDTH), lambda s: (s, 0)),
        out_shape=jax.ShapeDtypeStruct((m, POOL_WIDTH), BF16),
        scratch_shapes=scratch,
        compiler_params=_cparams(("arbitrary",)),
        name="pool2d" if on_grid else "pool1d",
    )(z, w_pool, pool_scale)


def _mm_out_ln_kernel(mo_ref, mp_ref, w_ref, x_ref, g1_ref, sh2_ref, sc2_ref, lg_ref, lb_ref,
                      x1_ref, h2_ref):
    for s in range(x_ref.shape[0] // LN_SUB):
        rows = pl.ds(s * LN_SUB, LN_SUB)
        m = _dot(mo_ref[rows, :], w_ref[:GLA_WIDTH, :]) + _dot(mp_ref[rows, :], w_ref[GLA_WIDTH:, :])
        x1 = _layer_norm(ALPHA * x_ref[rows, :] + g1_ref[...] * m, lg_ref[...], lb_ref[...])
        x1_ref[rows, :] = x1
        h2_ref[rows, :] = (x1 * (1.0 + sc2_ref[...]) + sh2_ref[...]).astype(BF16)


def _mm_out_ln_call(mo, mp, w_out, x, mod4, rows_per_mod, ln_g, ln_b, layer):
    m = x.shape[0]
    tm = min(512, m)
    row = pl.BlockSpec((tm, D_MODEL), lambda i: (i, 0))
    half = pl.BlockSpec((tm, GLA_WIDTH), lambda i: (i, 0))
    w_spec = pl.BlockSpec((None, D_MODEL, D_MODEL), lambda i: (layer, 0, 0), pipeline_mode=pl.Buffered(1))
    mod = lambda part: _mod_spec(layer, part, rows_per_mod, tm)
    return pl.pallas_call(
        _mm_out_ln_kernel,
        grid=(m // tm,),
        in_specs=[half, half, w_spec, row, mod(2), mod(3), mod(4),
                  _layer_vec_spec(layer, D_MODEL), _layer_vec_spec(layer, D_MODEL)],
        out_specs=[row, row],
        out_shape=[jax.ShapeDtypeStruct((m, D_MODEL), F32), jax.ShapeDtypeStruct((m, D_MODEL), BF16)],
        compiler_params=_cparams(("arbitrary",)),
        name="mm_out_ln",
    )(mo, mp, w_out, x, mod4, mod4, mod4, ln_g, ln_b)


def _mm_gu_kernel(h_ref, wg_ref, wu_ref, o_ref):
    h = h_ref[...]
    for c in range(o_ref.shape[1] // GU_SUB):
        cols = pl.ds(c * GU_SUB, GU_SUB)
        gate = _dot(h, wg_ref[:, cols].astype(BF16))
        up = _dot(h, wu_ref[:, cols].astype(BF16))
        o_ref[:, cols] = (jax.nn.silu(gate) * up).astype(BF16)


def _mm_gu_call(h, w_gu, layer):
    m = h.shape[0]
    tm = min(1024, m)
    tn = 512
    nj = D_FF // tn
    return pl.pallas_call(
        _mm_gu_kernel,
        grid=(nj, m // tm),
        in_specs=[
            pl.BlockSpec((tm, D_MODEL), lambda j, i: (i, 0)),
            pl.BlockSpec((None, D_MODEL, tn), lambda j, i: (layer, 0, j)),
            pl.BlockSpec((None, D_MODEL, tn), lambda j, i: (layer, 0, nj + j)),
        ],
        out_specs=pl.BlockSpec((tm, tn), lambda j, i: (i, j)),
        out_shape=jax.ShapeDtypeStruct((m, D_FF), BF16),
        compiler_params=_cparams(("arbitrary", "arbitrary")),
        name="mm_gu",
    )(h, w_gu, w_gu)


def _mm_down_kernel(*refs, final_ln):
    a_ref, w_ref, x1_ref, g2_ref = refs[:4]
    if final_ln:
        lg_ref, lb_ref, y_ref = refs[4:]
    else:
        (y_ref,) = refs[4:]
    for s in range(y_ref.shape[0] // LN_SUB):
        rows = pl.ds(s * LN_SUB, LN_SUB)
        y = ALPHA * x1_ref[rows, :] + g2_ref[...] * _dot(a_ref[rows, :], w_ref[...])
        y_ref[rows, :] = _layer_norm(y, lg_ref[...], lb_ref[...]) if final_ln else y


def _mm_down_call(act, w_down, x1, mod4, rows_per_mod, ln_g, ln_b, layer):
    m = x1.shape[0]
    tm = min(512, m)
    final_ln = layer + 1 == DEPTH
    row = pl.BlockSpec((tm, D_MODEL), lambda i: (i, 0))
    in_specs = [
        pl.BlockSpec((tm, D_FF), lambda i: (i, 0)),
        pl.BlockSpec((None, D_FF, D_MODEL), lambda i: (layer, 0, 0), pipeline_mode=pl.Buffered(1)),
        row,
        _mod_spec(layer, 5, rows_per_mod, tm),
    ]
    args = [act, w_down, x1, mod4]
    if final_ln:
        in_specs += [_layer_vec_spec(layer, D_MODEL), _layer_vec_spec(layer, D_MODEL)]
        args += [ln_g, ln_b]
    return pl.pallas_call(
        functools.partial(_mm_down_kernel, final_ln=final_ln),
        grid=(m // tm,),
        in_specs=in_specs,
        out_specs=row,
        out_shape=jax.ShapeDtypeStruct((m, D_MODEL), F32),
        compiler_params=_cparams(("arbitrary",)),
        name="mm_down_ln" if final_ln else "mm_down",
    )(*args)


def _prep_weights(w_in, w_a2, b_a, w_pool, w_out, w_down):
    depth = w_in.shape[0]
    gate_pad = jnp.zeros((depth, D_MODEL, Z_GATE_PAD - 2 * GLA_GATE_RANK), w_in.dtype)
    w_in_r = jnp.concatenate(
        [w_in[:, :, :W_AF], w_in[:, :, W_P:], w_in[:, :, W_AF:W_P], gate_pad], axis=-1).astype(BF16)
    wgate = jnp.zeros((depth, Z_GATE_PAD, GLA_HEADS, 2, GLA_DK), F32)
    per_head = lambda t: t.reshape(depth, GLA_GATE_RANK, GLA_HEADS, GLA_DK)
    wgate = wgate.at[:, :GLA_GATE_RANK, :, 0, :].set(per_head(w_a2[:, 0]))
    wgate = wgate.at[:, GLA_GATE_RANK:2 * GLA_GATE_RANK, :, 1, :].set(per_head(w_a2[:, 1]))
    wgate = wgate.reshape(depth, Z_GATE_PAD, 2 * GLA_KEY_WIDTH)
    bgate = b_a.reshape(depth, 2, GLA_HEADS, GLA_DK).transpose(0, 2, 1, 3).reshape(depth, 1, 2 * GLA_KEY_WIDTH)
    return (w_in_r, wgate.astype(BF16), bgate, w_pool.astype(BF16), w_out.astype(BF16),
            w_down.astype(BF16))


def kernel(x_prompt, x_sample, state_gla, c, c_ctx, ln_in_g, ln_in_b, w_ada, b_ada, w_in, w_a2, b_a,
           gla_norm_g, w_pool, pool_scale, w_out, ln1_g, ln1_b, w_gu, w_down, ln2_g, ln2_b):
    nb_p, len_p, _ = x_prompt.shape
    nb_s, len_s, _ = x_sample.shape
    w_in_r, wgate, bgate, w_pool_b, w_out_b, w_down_b = _prep_weights(
        w_in, w_a2, b_a, w_pool, w_out, w_down)
    gnorm = gla_norm_g.reshape(DEPTH, 1, GLA_WIDTH)
    pscale = pool_scale.reshape(DEPTH, 1, POOL_WIDTH)
    ln1 = (ln1_g.reshape(DEPTH, 1, D_MODEL), ln1_b.reshape(DEPTH, 1, D_MODEL))
    ln2 = (ln2_g.reshape(DEPTH, 1, D_MODEL), ln2_b.reshape(DEPTH, 1, D_MODEL))

    n_mod = -(-(1 + nb_s) // 8) * 8
    c_all = jnp.concatenate(
        [c_ctx[None, :], c, jnp.zeros((n_mod - 1 - nb_s, D_MODEL), c.dtype)], axis=0)
    mod4 = _ada_call(c_all, w_ada, b_ada).reshape(DEPTH, n_mod, 1, 6 * D_MODEL)

    streams = [
        (x_prompt.reshape(nb_p * len_p, D_MODEL), None, nb_p, len_p, False),
        (x_sample.reshape(nb_s * len_s, D_MODEL), len_s, nb_s, len_s, True),
    ]
    results = []
    state_buf = jnp.zeros((DEPTH, 2, nb_p, GLA_HEADS, GLA_DK, GLA_DV), F32)
    ln_in = (ln_in_g.reshape(1, 1, D_MODEL), ln_in_b.reshape(1, 1, D_MODEL))
    for y, rows_per_mod, nseq, seq_len, on_grid in streams:
        for l in range(DEPTH):
            ln_g, ln_b, ln_layer = (ln_in[0], ln_in[1], 0) if l == 0 else (ln2[0], ln2[1], l - 1)
            z, x = _ln_mm_in_call(y, ln_g, ln_b, ln_layer, mod4, rows_per_mod, w_in_r, l)
            if on_grid:
                (mo,) = _gla_call(z, wgate, bgate, gnorm, l, nseq, seq_len, s0=state_gla)
            else:
                mo, state_buf = _gla_call(z, wgate, bgate, gnorm, l, nseq, seq_len, state_buf=state_buf)
            mp = _pool_call(z, w_pool_b, pscale, l, nseq, seq_len, on_grid)
            x1, h2 = _mm_out_ln_call(mo, mp, w_out_b, x, mod4, rows_per_mod, ln1[0], ln1[1], l)
            act = _mm_gu_call(h2, w_gu, l)
            y = _mm_down_call(act, w_down_b, x1, mod4, rows_per_mod, ln2[0], ln2[1], l)
        results.append(y)
    y_prompt = results[0].reshape(nb_p, len_p, D_MODEL)
    y_sample = results[1].reshape(nb_s, len_s, D_MODEL)
    return (y_prompt, y_sample, state_buf.astype(x_prompt.dtype))
```

```python
import functools

import numpy as np
import jax
import jax.numpy as jnp
from jax import lax
from jax.experimental import pallas as pl
from jax.experimental.pallas import tpu as pltpu

F32 = jnp.float32
BF16 = jnp.bfloat16

D_MODEL = 2048
DEPTH = 4
GLA_HEADS = 4
GLA_DV = 256
GLA_DK = 128
GLA_KEY_WIDTH = GLA_HEADS * GLA_DK
GLA_WIDTH = GLA_HEADS * GLA_DV
POOL_WIDTH = 1024
GLA_GATE_RANK = 16
GLA_GATE_TEMP = 16.0
GLA_CHUNK = 64
POOL_WINDOWS = (2, 4, 8, 16)
POOL_GROUP_DIM = 256
GRID_W = 64
D_FF = 5632
ALPHA = (2 * DEPTH) ** 0.25
LN_EPS = 1e-5
Q_SCALE = GLA_DK ** -0.5

Z_Q = 0
Z_K = Z_Q + GLA_KEY_WIDTH
Z_V = Z_K + GLA_KEY_WIDTH
Z_G = Z_V + GLA_WIDTH
Z_P = Z_G + GLA_WIDTH
Z_GATE = Z_P + POOL_WIDTH
Z_GATE_PAD = 128
Z_WIDTH = Z_GATE + Z_GATE_PAD
W_AF = 2 * GLA_KEY_WIDTH + 2 * GLA_WIDTH
W_P = W_AF + 2 * GLA_GATE_RANK

P1_BLK = 256
POOL_BLK = 256
GU_SUB = 256
LN_SUB = 128
CHUNK_SHIFT = GLA_CHUNK.bit_length() - 1
GRID_SHIFT = GRID_W.bit_length() - 1
V7X_VMEM_LIMIT_BYTES = 58 * 2 ** 20


def _log_sigmoid(x):
    return jnp.minimum(x, 0.0) - jnp.log1p(jnp.exp(-jnp.abs(x)))


def _dot(a, b):
    return jnp.dot(a, b, preferred_element_type=F32)


def _dot_nt(a, b):
    return lax.dot_general(a, b, (((1,), (1,)), ((), ())), preferred_element_type=F32)


def _band_sum(band, x):
    hi = x.astype(BF16)
    r1 = x - hi.astype(F32)
    mid = r1.astype(BF16)
    lo = (r1 - mid.astype(F32)).astype(BF16)
    return _dot(band, hi) + _dot(band, mid) + _dot(band, lo)


def _cparams(semantics):
    return pltpu.CompilerParams(dimension_semantics=semantics, vmem_limit_bytes=V7X_VMEM_LIMIT_BYTES)


def _layer_norm(y, g, b):
    mu = jnp.mean(y, axis=-1, keepdims=True)
    yc = y - mu
    var = jnp.mean(yc * yc, axis=-1, keepdims=True)
    return yc * lax.rsqrt(var + LN_EPS) * g + b


def _mod_spec(layer, part, rows_per_mod, tm):
    if rows_per_mod is None:
        return pl.BlockSpec((None, None, 1, D_MODEL), lambda i, *_: (layer, 0, 0, part))
    return pl.BlockSpec((None, None, 1, D_MODEL),
                        lambda i, *_: (layer, 1 + (i * tm) // rows_per_mod, 0, part))


def _layer_vec_spec(layer, width):
    return pl.BlockSpec((None, 1, width), lambda *_: (layer, 0, 0))


def _ada_kernel(c_ref, w_ref, b_ref, o_ref):
    c = c_ref[...]
    s = jax.nn.silu(c).astype(BF16)
    o_ref[...] = _dot(s, w_ref[...].astype(BF16)) + b_ref[...]


def _ada_call(c_all, w_ada, b_ada):
    rows = c_all.shape[0]
    tn = 1024
    nj = (6 * D_MODEL) // tn
    return pl.pallas_call(
        _ada_kernel,
        grid=(DEPTH, nj),
        in_specs=[
            pl.BlockSpec((rows, D_MODEL), lambda l, j: (0, 0)),
            pl.BlockSpec((None, D_MODEL, tn), lambda l, j: (l, 0, j)),
            pl.BlockSpec((None, 1, tn), lambda l, j: (l, 0, j)),
        ],
        out_specs=pl.BlockSpec((None, rows, tn), lambda l, j: (l, 0, j)),
        out_shape=jax.ShapeDtypeStruct((DEPTH, rows, 6 * D_MODEL), F32),
        compiler_params=_cparams(("arbitrary", "arbitrary")),
        name="ada_mod",
    )(c_all, w_ada, b_ada.reshape(DEPTH, 1, 6 * D_MODEL))


def _ln_mm_in_kernel(y_ref, lg_ref, lb_ref, sh_ref, sc_ref, w_ref, z_ref, x_ref):
    for s in range(y_ref.shape[0] // LN_SUB):
        rows = pl.ds(s * LN_SUB, LN_SUB)
        x = _layer_norm(y_ref[rows, :], lg_ref[...], lb_ref[...])
        x_ref[rows, :] = x
        h = (x * (1.0 + sc_ref[...]) + sh_ref[...]).astype(BF16)
        z_ref[rows, :] = _dot(h, w_ref[...])


def _ln_mm_in_call(y, ln_g, ln_b, ln_layer, mod4, rows_per_mod, w, layer):
    m = y.shape[0]
    tm = min(512, m)
    row = pl.BlockSpec((tm, D_MODEL), lambda i: (i, 0))
    return pl.pallas_call(
        _ln_mm_in_kernel,
        grid=(m // tm,),
        in_specs=[
            row, _layer_vec_spec(ln_layer, D_MODEL), _layer_vec_spec(ln_layer, D_MODEL),
            _mod_spec(layer, 0, rows_per_mod, tm), _mod_spec(layer, 1, rows_per_mod, tm),
            pl.BlockSpec((None, D_MODEL, Z_WIDTH), lambda i: (layer, 0, 0), pipeline_mode=pl.Buffered(1)),
        ],
        out_specs=[pl.BlockSpec((tm, Z_WIDTH), lambda i: (i, 0)), row],
        out_shape=[jax.ShapeDtypeStruct((m, Z_WIDTH), F32), jax.ShapeDtypeStruct((m, D_MODEL), F32)],
        compiler_params=_cparams(("arbitrary",)),
        name="ln_mm_in",
    )(y, ln_g, ln_b, mod4, mod4, w)


def _gla_kernel(*refs, seq_len, has_s0, emit_state):
    (q_ref, k_ref, v_ref, g_ref, zg_ref, wg_ref, bg_ref, gn_ref, tl_ref) = refs[:9]
    pos = 9
    if has_s0:
        s0f_ref, s0b_ref = refs[pos:pos + 2]
        pos += 2
    if emit_state:
        pos += 1
    mo_ref = refs[pos]
    pos += 1
    if emit_state:
        st_ref = refs[pos]
        pos += 1
    qef, qeb, o_ref, u_ref, dec_ref = refs[pos:]

    n_blk = seq_len // P1_BLK
    n_chunks = seq_len // GLA_CHUNK
    c = GLA_CHUNK
    dk = GLA_DK

    brow = lax.broadcasted_iota(jnp.int32, (P1_BLK, P1_BLK), 0)
    bcol = lax.broadcasted_iota(jnp.int32, (P1_BLK, P1_BLK), 1)
    same_chunk = (brow >> CHUNK_SHIFT) == (bcol >> CHUNK_SHIFT)
    mask_f = jnp.logical_and(same_chunk, bcol <= brow)
    mask_b = jnp.logical_and(same_chunk, bcol >= brow)
    row_chunk = lax.broadcasted_iota(jnp.int32, (P1_BLK, dk), 0) >> CHUNK_SHIFT

    def phase_a(i, carry):
        r = pl.multiple_of(i * P1_BLK, P1_BLK)
        rows = pl.ds(r, P1_BLK)
        zg = zg_ref[rows, :].astype(BF16)
        la = _log_sigmoid(_dot(zg, wg_ref[...]) + bg_ref[...]) * (1.0 / GLA_GATE_TEMP)
        pre = _band_sum(tl_ref[...], la)
        n_sub = P1_BLK // c
        tot = jnp.concatenate(
            [jnp.broadcast_to(pre[(s + 1) * c - 1:(s + 1) * c, :], (c, 2 * dk)) for s in range(n_sub)], axis=0)
        b_f = pre[:, :dk]
        tot_f = tot[:, :dk]
        tot_b = tot[:, dk:]
        b_b = tot_b - pre[:, dk:] + la[:, dk:]
        q = q_ref[rows, :] * Q_SCALE
        k = k_ref[rows, :]
        qe_f = (q * jnp.exp(b_f)).astype(BF16)
        qe_b = (q * jnp.exp(b_b)).astype(BF16)
        qef[rows, :] = qe_f
        qeb[rows, :] = qe_b
        ke_f = (k * jnp.exp(-b_f)).astype(BF16)
        ke_b = (k * jnp.exp(-b_b)).astype(BF16)
        a = (jnp.where(mask_f, _dot_nt(qe_f, ke_f), 0.0) + jnp.where(mask_b, _dot_nt(qe_b, ke_b), 0.0))
        v = v_ref[rows, :]
        o_ref[rows, :] = _dot(a.astype(BF16), v.astype(BF16))
        kd_f = k * jnp.exp(tot_f - b_f)
        kd_b = k * jnp.exp(tot_b - b_b)
        kd_cols = [jnp.where(row_chunk == s, kd, 0.0).astype(BF16) for kd in (kd_f, kd_b) for s in range(n_sub)]
        u = _dot(v.T.astype(BF16), jnp.concatenate(kd_cols, axis=-1))
        for s in range(n_sub):
            ch = n_sub * i + s
            u_ref[ch] = u[:, s * dk:(s + 1) * dk]
            u_ref[n_chunks + ch] = u[:, (n_sub + s) * dk:(n_sub + s + 1) * dk]
            dec = jnp.exp(tot[s * c:s * c + 8, :])
            dec_ref[pl.ds(pl.multiple_of(8 * ch, 8), 8), :] = dec[:, :dk]
            dec_ref[pl.ds(pl.multiple_of(8 * (n_chunks + ch), 8), 8), :] = dec[:, dk:]
        return carry

    lax.fori_loop(0, n_blk, phase_a, 0, unroll=4 if n_blk % 4 == 0 else 1)

    def fwd_chunk(ch, st):
        rows = pl.ds(pl.multiple_of(ch * c, c), c)
        o_ref[rows, :] += _dot_nt(qef[rows, :], st.astype(BF16))
        return st * dec_ref[pl.ds(pl.multiple_of(8 * ch, 8), 1), :] + u_ref[ch]

    def bwd_chunk(j, st):
        ch = n_chunks - 1 - j
        rows = pl.ds(pl.multiple_of(ch * c, c), c)
        o = o_ref[rows, :] + _dot_nt(qeb[rows, :], st.astype(BF16))
        ms = jnp.mean(o * o, axis=-1, keepdims=True)
        y = o * lax.rsqrt(ms + LN_EPS) * gn_ref[...]
        mo_ref[rows, :] = (y * jax.nn.silu(g_ref[rows, :])).astype(BF16)
        return st * dec_ref[pl.ds(pl.multiple_of(8 * (n_chunks + ch), 8), 1), :] + u_ref[n_chunks + ch]

    if has_s0:
        st0_f = s0f_ref[...].T
        st0_b = s0b_ref[...].T
    else:
        st0_f = jnp.zeros((GLA_DV, dk), F32)
        st0_b = st0_f
    unroll = True if n_chunks <= 4 else 16
    st_f = lax.fori_loop(0, n_chunks, fwd_chunk, st0_f, unroll=unroll)
    st_b = lax.fori_loop(0, n_chunks, bwd_chunk, st0_b, unroll=unroll)
    if emit_state:
        st_ref[0] = st_f.T
        st_ref[1] = st_b.T


def _chunk_prefix_matrix():
    i = np.arange(P1_BLK)
    same = (i[:, None] >> CHUNK_SHIFT) == (i[None, :] >> CHUNK_SHIFT)
    return jnp.asarray(np.logical_and(same, i[None, :] <= i[:, None]).astype(np.float32), BF16)


def _gla_call(z, wgate, bgate, gnorm, layer, nseq, seq_len, s0=None, state_buf=None):
    m = nseq * seq_len
    tl = _chunk_prefix_matrix()
    has_s0 = s0 is not None
    emit_state = state_buf is not None
    kw, vw = GLA_DK, GLA_DV
    in_specs = [
        pl.BlockSpec((seq_len, kw), lambda s, h: (s, Z_Q // kw + h)),
        pl.BlockSpec((seq_len, kw), lambda s, h: (s, Z_K // kw + h)),
        pl.BlockSpec((seq_len, vw), lambda s, h: (s, Z_V // vw + h)),
        pl.BlockSpec((seq_len, vw), lambda s, h: (s, Z_G // vw + h)),
        pl.BlockSpec((seq_len, Z_GATE_PAD), lambda s, h: (s, Z_GATE // Z_GATE_PAD)),
        pl.BlockSpec((None, Z_GATE_PAD, 2 * kw), lambda s, h: (layer, 0, h)),
        pl.BlockSpec((None, 1, 2 * kw), lambda s, h: (layer, 0, h)),
        pl.BlockSpec((None, 1, vw), lambda s, h: (layer, 0, h)),
        pl.BlockSpec((P1_BLK, P1_BLK), lambda s, h: (0, 0)),
    ]
    args = [z, z, z, z, z, wgate, bgate, gnorm, tl]
    if has_s0:
        in_specs += [
            pl.BlockSpec((None, None, None, None, kw, vw), lambda s, h: (layer, 0, s, h, 0, 0)),
            pl.BlockSpec((None, None, None, None, kw, vw), lambda s, h: (layer, 1, s, h, 0, 0)),
        ]
        args += [s0, s0]
    aliases = {}
    if emit_state:
        aliases = {len(args): 1}
        in_specs.append(pl.BlockSpec(memory_space=pl.ANY))
        args.append(state_buf)
    out_specs = [pl.BlockSpec((seq_len, vw), lambda s, h: (s, h))]
    out_shape = [jax.ShapeDtypeStruct((m, GLA_WIDTH), BF16)]
    if emit_state:
        out_specs.append(pl.BlockSpec((None, 2, None, None, kw, vw), lambda s, h: (layer, 0, s, h, 0, 0)))
        out_shape.append(jax.ShapeDtypeStruct((DEPTH, 2, nseq, GLA_HEADS, kw, vw), F32))
    n_chunks = seq_len // GLA_CHUNK
    scratch = [
        pltpu.VMEM((seq_len, kw), BF16),
        pltpu.VMEM((seq_len, kw), BF16),
        pltpu.VMEM((seq_len, vw), F32),
        pltpu.VMEM((2 * n_chunks, vw, kw), F32),
        pltpu.VMEM((2 * n_chunks * 8, kw), F32),
    ]
    return pl.pallas_call(
        functools.partial(_gla_kernel, seq_len=seq_len, has_s0=has_s0, emit_state=emit_state),
        grid=(nseq, GLA_HEADS),
        in_specs=in_specs,
        out_specs=out_specs,
        out_shape=out_shape,
        scratch_shapes=scratch,
        input_output_aliases=aliases,
        compiler_params=_cparams(("arbitrary", "arbitrary")),
        name="gla_state" if emit_state else "gla",
    )(*args)


def _window_count(idx, n, w):
    lo = jnp.maximum(idx - w // 2, 0)
    hi = jnp.minimum(idx - w // 2 + w, n)
    return hi - lo


def _pool_project(pooled, gi, w_ref, sc_ref, o_ref, rows):
    cols = slice(gi * POOL_GROUP_DIM, (gi + 1) * POOL_GROUP_DIM)
    y = _dot(pooled.astype(BF16), w_ref[gi]) * sc_ref[:, cols]
    o_ref[rows, cols] = y.astype(BF16)


def _shifted(x, pos, n, d):
    rolled = pltpu.roll(x, (-d) % x.shape[0], 0)
    return jnp.where(jnp.logical_and(pos + d >= 0, pos + d < n), rolled, 0.0)


def _window_sum(x, pos, n, w):
    fwd = bwd = x
    s = 1
    while s < w // 2:
        fwd = fwd + _shifted(fwd, pos, n, s)
        bwd = bwd + _shifted(bwd, pos, n, -s)
        s *= 2
    return fwd + _shifted(bwd, pos, n, -1)


def _pool1d_kernel(p_ref, w_ref, sc_ref, o_ref, *, seq_len):
    t = lax.broadcasted_iota(jnp.int32, (seq_len, 1), 0)
    for gi, w in enumerate(POOL_WINDOWS):
        x = p_ref[:, gi * POOL_GROUP_DIM:(gi + 1) * POOL_GROUP_DIM]
        cnt = _window_count(t, seq_len, w).astype(F32)
        _pool_project(_window_sum(x, t, seq_len, w) / cnt - x, gi, w_ref, sc_ref, o_ref, slice(None))


def _pool2d_kernel(p_ref, w_ref, sc_ref, o_ref, cs_ref, *, seq_len):
    n_rows = seq_len // GRID_W
    pad = (max(POOL_WINDOWS) // 2) * GRID_W
    n_blk = seq_len // POOL_BLK
    zeros = jnp.zeros((pad, POOL_GROUP_DIM), F32)
    cs_ref[pl.ds(0, pad), :] = zeros
    cs_ref[pl.ds(pad + seq_len, pad), :] = zeros
    local = lax.broadcasted_iota(jnp.int32, (POOL_BLK, 1), 0)
    col = local & (GRID_W - 1)
    for gi, w in enumerate(POOL_WINDOWS):
        cols = slice(gi * POOL_GROUP_DIM, (gi + 1) * POOL_GROUP_DIM)

        def col_sums(i, carry, w=w, cols=cols):
            r = pl.multiple_of(i * POOL_BLK, POOL_BLK)
            acc = _window_sum(p_ref[pl.ds(r, POOL_BLK), cols], col, GRID_W, w)
            cs_ref[pl.ds(pl.multiple_of(r + pad, GRID_W), POOL_BLK), :] = acc
            return carry

        lax.fori_loop(0, n_blk, col_sums, 0)

        def row_sums(i, carry, w=w, cols=cols, gi=gi):
            r = pl.multiple_of(i * POOL_BLK, POOL_BLK)
            acc = jnp.zeros((POOL_BLK, POOL_GROUP_DIM), F32)
            for j in range(w):
                off = pad + (j - w // 2) * GRID_W
                acc = acc + cs_ref[pl.ds(pl.multiple_of(r + off, GRID_W), POOL_BLK), :]
            grow = (r + local) >> GRID_SHIFT
            cnt = (_window_count(grow, n_rows, w) * _window_count(col, GRID_W, w)).astype(F32)
            x = p_ref[pl.ds(r, POOL_BLK), cols]
            _pool_project(acc / cnt - x, gi, w_ref, sc_ref, o_ref, pl.ds(r, POOL_BLK))
            return carry

        lax.fori_loop(0, n_blk, row_sums, 0, unroll=2)


def _pool_call(z, w_pool, pool_scale, layer, nseq, seq_len, on_grid):
    m = nseq * seq_len
    if on_grid:
        body = functools.partial(_pool2d_kernel, seq_len=seq_len)
        pad = (max(POOL_WINDOWS) // 2) * GRID_W
        scratch = [pltpu.VMEM((seq_len + 2 * pad, POOL_GROUP_DIM), F32)]
    else:
        body = functools.partial(_pool1d_kernel, seq_len=seq_len)
        scratch = []
    n_groups = len(POOL_WINDOWS)
    return pl.pallas_call(
        body,
        grid=(nseq,),
        in_specs=[
            pl.BlockSpec((seq_len, POOL_WIDTH), lambda s: (s, Z_P // POOL_WIDTH)),
            pl.BlockSpec((None, n_groups, POOL_GROUP_DIM, POOL_GROUP_DIM), lambda s: (layer, 0, 0, 0)),
            _layer_vec_spec(layer, POOL_WIDTH),
        ],
        out_specs=pl.BlockSpec((seq_len, POOL_WIDTH), lambda s: (s, 0)),
        out_shape=jax.ShapeDtypeStruct((m, POOL_WIDTH), BF16),
        scratch_shapes=scratch,
        compiler_params=_cparams(("arbitrary",)),
        name="pool2d" if on_grid else "pool1d",
    )(z, w_pool, pool_scale)


def _mm_out_ln_kernel(mo_ref, mp_ref, w_ref, x_ref, g1_ref, sh2_ref, sc2_ref, lg_ref, lb_ref,
                      x1_ref, h2_ref):
    for s in range(x_ref.shape[0] // LN_SUB):
        rows = pl.ds(s * LN_SUB, LN_SUB)
        m = _dot(mo_ref[rows, :], w_ref[:GLA_WIDTH, :]) + _dot(mp_ref[rows, :], w_ref[GLA_WIDTH:, :])
        x1 = _layer_norm(ALPHA * x_ref[rows, :] + g1_ref[...] * m, lg_ref[...], lb_ref[...])
        x1_ref[rows, :] = x1
        h2_ref[rows, :] = (x1 * (1.0 + sc2_ref[...]) + sh2_ref[...]).astype(BF16)


def _mm_out_ln_call(mo, mp, w_out, x, mod4, rows_per_mod, ln_g, ln_b, layer):
    m = x.shape[0]
    tm = min(512, m)
    row = pl.BlockSpec((tm, D_MODEL), lambda i: (i, 0))
    half = pl.BlockSpec((tm, GLA_WIDTH), lambda i: (i, 0))
    w_spec = pl.BlockSpec((None, D_MODEL, D_MODEL), lambda i: (layer, 0, 0), pipeline_mode=pl.Buffered(1))
    mod = lambda part: _mod_spec(layer, part, rows_per_mod, tm)
    return pl.pallas_call(
        _mm_out_ln_kernel,
        grid=(m // tm,),
        in_specs=[half, half, w_spec, row, mod(2), mod(3), mod(4),
                  _layer_vec_spec(layer, D_MODEL), _layer_vec_spec(layer, D_MODEL)],
        out_specs=[row, row],
        out_shape=[jax.ShapeDtypeStruct((m, D_MODEL), F32), jax.ShapeDtypeStruct((m, D_MODEL), BF16)],
        compiler_params=_cparams(("arbitrary",)),
        name="mm_out_ln",
    )(mo, mp, w_out, x, mod4, mod4, mod4, ln_g, ln_b)


def _mm_gu_kernel(h_ref, wg_ref, wu_ref, o_ref):
    h = h_ref[...]
    for c in range(o_ref.shape[1] // GU_SUB):
        cols = pl.ds(c * GU_SUB, GU_SUB)
        gate = _dot(h, wg_ref[:, cols].astype(BF16))
        up = _dot(h, wu_ref[:, cols].astype(BF16))
        o_ref[:, cols] = (jax.nn.silu(gate) * up).astype(BF16)


def _mm_gu_call(h, w_gu, layer):
    m = h.shape[0]
    tm = min(1024, m)
    tn = 512
    nj = D_FF // tn
    return pl.pallas_call(
        _mm_gu_kernel,
        grid=(nj, m // tm),
        in_specs=[
            pl.BlockSpec((tm, D_MODEL), lambda j, i: (i, 0)),
            pl.BlockSpec((None, D_MODEL, tn), lambda j, i: (layer, 0, j)),
            pl.BlockSpec((None, D_MODEL, tn), lambda j, i: (layer, 0, nj + j)),
        ],
        out_specs=pl.BlockSpec((tm, tn), lambda j, i: (i, j)),
        out_shape=jax.ShapeDtypeStruct((m, D_FF), BF16),
        compiler_params=_cparams(("arbitrary", "arbitrary")),
        name="mm_gu",
    )(h, w_gu, w_gu)


def _mm_down_kernel(*refs, final_ln):
    a_ref, w_ref, x1_ref, g2_ref = refs[:4]
    if final_ln:
        lg_ref, lb_ref, y_ref = refs[4:]
    else:
        (y_ref,) = refs[4:]
    for s in range(y_ref.shape[0] // LN_SUB):
        rows = pl.ds(s * LN_SUB, LN_SUB)
        y = ALPHA * x1_ref[rows, :] + g2_ref[...] * _dot(a_ref[rows, :], w_ref[...])
        y_ref[rows, :] = _layer_norm(y, lg_ref[...], lb_ref[...]) if final_ln else y


def _mm_down_call(act, w_down, x1, mod4, rows_per_mod, ln_g, ln_b, layer):
    m = x1.shape[0]
    tm = min(512, m)
    final_ln = layer + 1 == DEPTH
    row = pl.BlockSpec((tm, D_MODEL), lambda i: (i, 0))
    in_specs = [
        pl.BlockSpec((tm, D_FF), lambda i: (i, 0)),
        pl.BlockSpec((None, D_FF, D_MODEL), lambda i: (layer, 0, 0), pipeline_mode=pl.Buffered(1)),
        row,
        _mod_spec(layer, 5, rows_per_mod, tm),
    ]
    args = [act, w_down, x1, mod4]
    if final_ln:
        in_specs += [_layer_vec_spec(layer, D_MODEL), _layer_vec_spec(layer, D_MODEL)]
        args += [ln_g, ln_b]
    return pl.pallas_call(
        functools.partial(_mm_down_kernel, final_ln=final_ln),
        grid=(m // tm,),
        in_specs=in_specs,
        out_specs=row,
        out_shape=jax.ShapeDtypeStruct((m, D_MODEL), F32),
        compiler_params=_cparams(("arbitrary",)),
        name="mm_down_ln" if final_ln else "mm_down",
    )(*args)


def _prep_weights(w_in, w_a2, b_a, w_pool, w_out, w_down):
    depth = w_in.shape[0]
    gate_pad = jnp.zeros((depth, D_MODEL, Z_GATE_PAD - 2 * GLA_GATE_RANK), w_in.dtype)
    w_in_r = jnp.concatenate(
        [w_in[:, :, :W_AF], w_in[:, :, W_P:], w_in[:, :, W_AF:W_P], gate_pad], axis=-1).astype(BF16)
    wgate = jnp.zeros((depth, Z_GATE_PAD, GLA_HEADS, 2, GLA_DK), F32)
    per_head = lambda t: t.reshape(depth, GLA_GATE_RANK, GLA_HEADS, GLA_DK)
    wgate = wgate.at[:, :GLA_GATE_RANK, :, 0, :].set(per_head(w_a2[:, 0]))
    wgate = wgate.at[:, GLA_GATE_RANK:2 * GLA_GATE_RANK, :, 1, :].set(per_head(w_a2[:, 1]))
    wgate = wgate.reshape(depth, Z_GATE_PAD, 2 * GLA_KEY_WIDTH)
    bgate = b_a.reshape(depth, 2, GLA_HEADS, GLA_DK).transpose(0, 2, 1, 3).reshape(depth, 1, 2 * GLA_KEY_WIDTH)
    return (w_in_r, wgate.astype(BF16), bgate, w_pool.astype(BF16), w_out.astype(BF16),
            w_down.astype(BF16))


def kernel(x_prompt, x_sample, state_gla, c, c_ctx, ln_in_g, ln_in_b, w_ada, b_ada, w_in, w_a2, b_a,
           gla_norm_g, w_pool, pool_scale, w_out, ln1_g, ln1_b, w_gu, w_down, ln2_g, ln2_b):
    nb_p, len_p, _ = x_prompt.shape
    nb_s, len_s, _ = x_sample.shape
    w_in_r, wgate, bgate, w_pool_b, w_out_b, w_down_b = _prep_weights(
        w_in, w_a2, b_a, w_pool, w_out, w_down)
    gnorm = gla_norm_g.reshape(DEPTH, 1, GLA_WIDTH)
    pscale = pool_scale.reshape(DEPTH, 1, POOL_WIDTH)
    ln1 = (ln1_g.reshape(DEPTH, 1, D_MODEL), ln1_b.reshape(DEPTH, 1, D_MODEL))
    ln2 = (ln2_g.reshape(DEPTH, 1, D_MODEL), ln2_b.reshape(DEPTH, 1, D_MODEL))

    n_mod = -(-(1 + nb_s) // 8) * 8
    c_all = jnp.concatenate(
        [c_ctx[None, :], c, jnp.zeros((n_mod - 1 - nb_s, D_MODEL), c.dtype)], axis=0)
    mod4 = _ada_call(c_all, w_ada, b_ada).reshape(DEPTH, n_mod, 1, 6 * D_MODEL)

    streams = [
        (x_prompt.reshape(nb_p * len_p, D_MODEL), None, nb_p, len_p, False),
        (x_sample.reshape(nb_s * len_s, D_MODEL), len_s, nb_s, len_s, True),
    ]
    results = []
    state_buf = jnp.zeros((DEPTH, 2, nb_p, GLA_HEADS, GLA_DK, GLA_DV), F32)
    ln_in = (ln_in_g.reshape(1, 1, D_MODEL), ln_in_b.reshape(1, 1, D_MODEL))
    for y, rows_per_mod, nseq, seq_len, on_grid in streams:
        for l in range(DEPTH):
            ln_g, ln_b, ln_layer = (ln_in[0], ln_in[1], 0) if l == 0 else (ln2[0], ln2[1], l - 1)
            z, x = _ln_mm_in_call(y, ln_g, ln_b, ln_layer, mod4, rows_per_mod, w_in_r, l)
            if on_grid:
                (mo,) = _gla_call(z, wgate, bgate, gnorm, l, nseq, seq_len, s0=state_gla)
            else:
                mo, state_buf = _gla_call(z, wgate, bgate, gnorm, l, nseq, seq_len, state_buf=state_buf)
            mp = _pool_call(z, w_pool_b, pscale, l, nseq, seq_len, on_grid)
            x1, h2 = _mm_out_ln_call(mo, mp, w_out_b, x, mod4, rows_per_mod, ln1[0], ln1[1], l)
            act = _mm_gu_call(h2, w_gu, l)
            y = _mm_down_call(act, w_down_b, x1, mod4, rows_per_mod, ln2[0], ln2[1], l)
        results.append(y)
    y_prompt = results[0].reshape(nb_p, len_p, D_MODEL)
    y_sample = results[1].reshape(nb_s, len_s, D_MODEL)
    return (y_prompt, y_sample, state_buf.astype(x_prompt.dtype))
```

```python
import functools

import numpy as np
import jax
import jax.numpy as jnp
from jax import lax
from jax.experimental import pallas as pl
from jax.experimental.pallas import tpu as pltpu

F32 = jnp.float32
BF16 = jnp.bfloat16

D_MODEL = 2048
DEPTH = 4
GLA_HEADS = 4
GLA_DV = 256
GLA_DK = 128
GLA_KEY_WIDTH = GLA_HEADS * GLA_DK
GLA_WIDTH = GLA_HEADS * GLA_DV
POOL_WIDTH = 1024
GLA_GATE_RANK = 16
GLA_GATE_TEMP = 16.0
GLA_CHUNK = 64
POOL_WINDOWS = (2, 4, 8, 16)
POOL_GROUP_DIM = 256
GRID_W = 64
D_FF = 5632
ALPHA = (2 * DEPTH) ** 0.25
LN_EPS = 1e-5
Q_SCALE = GLA_DK ** -0.5

Z_Q = 0
Z_K = Z_Q + GLA_KEY_WIDTH
Z_V = Z_K + GLA_KEY_WIDTH
Z_G = Z_V + GLA_WIDTH
Z_P = Z_G + GLA_WIDTH
Z_GATE = Z_P + POOL_WIDTH
Z_GATE_PAD = 128
Z_WIDTH = Z_GATE + Z_GATE_PAD
W_AF = 2 * GLA_KEY_WIDTH + 2 * GLA_WIDTH
W_P = W_AF + 2 * GLA_GATE_RANK

P1_BLK = 256
GLA_STEP_ROWS = 1024
POOL_BLK = 256
GU_SUB = 256
LN_SUB = 128
CHUNK_SHIFT = GLA_CHUNK.bit_length() - 1
GRID_SHIFT = GRID_W.bit_length() - 1
V7X_VMEM_LIMIT_BYTES = 58 * 2 ** 20


def _log_sigmoid(x):
    return jnp.minimum(x, 0.0) - jnp.log1p(jnp.exp(-jnp.abs(x)))


def _dot(a, b):
    return jnp.dot(a, b, preferred_element_type=F32)


def _dot_nt(a, b):
    return lax.dot_general(a, b, (((1,), (1,)), ((), ())), preferred_element_type=F32)


def _band_sum(band, x):
    hi = x.astype(BF16)
    r1 = x - hi.astype(F32)
    mid = r1.astype(BF16)
    lo = (r1 - mid.astype(F32)).astype(BF16)
    return _dot(band, hi) + _dot(band, mid) + _dot(band, lo)


def _cparams(semantics):
    return pltpu.CompilerParams(dimension_semantics=semantics, vmem_limit_bytes=V7X_VMEM_LIMIT_BYTES)


def _layer_norm(y, g, b):
    mu = jnp.mean(y, axis=-1, keepdims=True)
    yc = y - mu
    var = jnp.mean(yc * yc, axis=-1, keepdims=True)
    return yc * lax.rsqrt(var + LN_EPS) * g + b


def _mod_spec(layer, part, rows_per_mod, tm):
    if rows_per_mod is None:
        return pl.BlockSpec((None, None, 1, D_MODEL), lambda i, *_: (layer, 0, 0, part))
    return pl.BlockSpec((None, None, 1, D_MODEL),
                        lambda i, *_: (layer, 1 + (i * tm) // rows_per_mod, 0, part))


def _layer_vec_spec(layer, width):
    return pl.BlockSpec((None, 1, width), lambda *_: (layer, 0, 0))


def _ada_kernel(c_ref, w_ref, b_ref, o_ref):
    c = c_ref[...]
    s = jax.nn.silu(c).astype(BF16)
    o_ref[...] = _dot(s, w_ref[...].astype(BF16)) + b_ref[...]


def _ada_call(c_all, w_ada, b_ada):
    rows = c_all.shape[0]
    tn = 1024
    nj = (6 * D_MODEL) // tn
    return pl.pallas_call(
        _ada_kernel,
        grid=(DEPTH, nj),
        in_specs=[
            pl.BlockSpec((rows, D_MODEL), lambda l, j: (0, 0)),
            pl.BlockSpec((None, D_MODEL, tn), lambda l, j: (l, 0, j)),
            pl.BlockSpec((None, 1, tn), lambda l, j: (l, 0, j)),
        ],
        out_specs=pl.BlockSpec((None, rows, tn), lambda l, j: (l, 0, j)),
        out_shape=jax.ShapeDtypeStruct((DEPTH, rows, 6 * D_MODEL), F32),
        compiler_params=_cparams(("arbitrary", "arbitrary")),
        name="ada_mod",
    )(c_all, w_ada, b_ada.reshape(DEPTH, 1, 6 * D_MODEL))


def _ln_mm_in_kernel(y_ref, lg_ref, lb_ref, sh_ref, sc_ref, w_ref, z_ref, x_ref):
    for s in range(y_ref.shape[0] // LN_SUB):
        rows = pl.ds(s * LN_SUB, LN_SUB)
        x = _layer_norm(y_ref[rows, :], lg_ref[...], lb_ref[...])
        x_ref[rows, :] = x
        h = (x * (1.0 + sc_ref[...]) + sh_ref[...]).astype(BF16)
        z_ref[rows, :] = _dot(h, w_ref[...])


def _ln_mm_in_call(y, ln_g, ln_b, ln_layer, mod4, rows_per_mod, w, layer):
    m = y.shape[0]
    tm = min(512, m)
    row = pl.BlockSpec((tm, D_MODEL), lambda i: (i, 0))
    return pl.pallas_call(
        _ln_mm_in_kernel,
        grid=(m // tm,),
        in_specs=[
            row, _layer_vec_spec(ln_layer, D_MODEL), _layer_vec_spec(ln_layer, D_MODEL),
            _mod_spec(layer, 0, rows_per_mod, tm), _mod_spec(layer, 1, rows_per_mod, tm),
            pl.BlockSpec((None, D_MODEL, Z_WIDTH), lambda i: (layer, 0, 0), pipeline_mode=pl.Buffered(1)),
        ],
        out_specs=[pl.BlockSpec((tm, Z_WIDTH), lambda i: (i, 0)), row],
        out_shape=[jax.ShapeDtypeStruct((m, Z_WIDTH), F32), jax.ShapeDtypeStruct((m, D_MODEL), F32)],
        compiler_params=_cparams(("arbitrary",)),
        name="ln_mm_in",
    )(y, ln_g, ln_b, mod4, mod4, w)


def _gla_kernel(*refs, seq_len, seqs, has_s0, emit_state):
    (q_ref, k_ref, v_ref, g_ref, zg_ref, wg_ref, bg_ref, gn_ref, tl_ref) = refs[:9]
    pos = 9
    if has_s0:
        s0f_ref, s0b_ref = refs[pos:pos + 2]
        pos += 2
    if emit_state:
        pos += 1
    mo_ref = refs[pos]
    pos += 1
    if emit_state:
        st_ref = refs[pos]
        pos += 1
    qef, qeb, o_ref, u_ref, dec_ref = refs[pos:]

    n_blk = seqs * seq_len // P1_BLK
    n_chunks = seqs * seq_len // GLA_CHUNK
    seq_chunks = seq_len // GLA_CHUNK
    c = GLA_CHUNK
    dk = GLA_DK

    brow = lax.broadcasted_iota(jnp.int32, (P1_BLK, P1_BLK), 0)
    bcol = lax.broadcasted_iota(jnp.int32, (P1_BLK, P1_BLK), 1)
    same_chunk = (brow >> CHUNK_SHIFT) == (bcol >> CHUNK_SHIFT)
    mask_f = jnp.logical_and(same_chunk, bcol <= brow)
    mask_b = jnp.logical_and(same_chunk, bcol >= brow)
    row_chunk = lax.broadcasted_iota(jnp.int32, (P1_BLK, dk), 0) >> CHUNK_SHIFT

    def phase_a(i, carry):
        r = pl.multiple_of(i * P1_BLK, P1_BLK)
        rows = pl.ds(r, P1_BLK)
        zg = zg_ref[rows, :].astype(BF16)
        la = _log_sigmoid(_dot(zg, wg_ref[...]) + bg_ref[...]) * (1.0 / GLA_GATE_TEMP)
        pre = _band_sum(tl_ref[...], la)
        n_sub = P1_BLK // c
        tot = jnp.concatenate(
            [jnp.broadcast_to(pre[(s + 1) * c - 1:(s + 1) * c, :], (c, 2 * dk)) for s in range(n_sub)], axis=0)
        b_f = pre[:, :dk]
        tot_f = tot[:, :dk]
        tot_b = tot[:, dk:]
        b_b = tot_b - pre[:, dk:] + la[:, dk:]
        q = q_ref[rows, :] * Q_SCALE
        k = k_ref[rows, :]
        qe_f = (q * jnp.exp(b_f)).astype(BF16)
        qe_b = (q * jnp.exp(b_b)).astype(BF16)
        qef[rows, :] = qe_f
        qeb[rows, :] = qe_b
        ke_f = (k * jnp.exp(-b_f)).astype(BF16)
        ke_b = (k * jnp.exp(-b_b)).astype(BF16)
        a = (jnp.where(mask_f, _dot_nt(qe_f, ke_f), 0.0) + jnp.where(mask_b, _dot_nt(qe_b, ke_b), 0.0))
        v = v_ref[rows, :]
        o_ref[rows, :] = _dot(a.astype(BF16), v.astype(BF16))
        kd_f = k * jnp.exp(tot_f - b_f)
        kd_b = k * jnp.exp(tot_b - b_b)
        kd_cols = [jnp.where(row_chunk == s, kd, 0.0).astype(BF16) for kd in (kd_f, kd_b) for s in range(n_sub)]
        u = _dot(v.T.astype(BF16), jnp.concatenate(kd_cols, axis=-1))
        for s in range(n_sub):
            ch = n_sub * i + s
            u_ref[ch] = u[:, s * dk:(s + 1) * dk]
            u_ref[n_chunks + ch] = u[:, (n_sub + s) * dk:(n_sub + s + 1) * dk]
            dec = jnp.exp(tot[s * c:s * c + 8, :])
            dec_ref[pl.ds(pl.multiple_of(8 * ch, 8), 8), :] = dec[:, :dk]
            dec_ref[pl.ds(pl.multiple_of(8 * (n_chunks + ch), 8), 8), :] = dec[:, dk:]
        return carry

    lax.fori_loop(0, n_blk, phase_a, 0, unroll=4 if n_blk % 4 == 0 else (2 if n_blk % 2 == 0 else 1))

    def fwd_chunk(ch, st):
        rows = pl.ds(pl.multiple_of(ch * c, c), c)
        o_ref[rows, :] += _dot_nt(qef[rows, :], st.astype(BF16))
        return st * dec_ref[pl.ds(pl.multiple_of(8 * ch, 8), 1), :] + u_ref[ch]

    def bwd_chunk(j, st, last):
        ch = last - j
        rows = pl.ds(pl.multiple_of(ch * c, c), c)
        o = o_ref[rows, :] + _dot_nt(qeb[rows, :], st.astype(BF16))
        ms = jnp.mean(o * o, axis=-1, keepdims=True)
        y = o * lax.rsqrt(ms + LN_EPS) * gn_ref[...]
        mo_ref[rows, :] = (y * jax.nn.silu(g_ref[rows, :])).astype(BF16)
        return st * dec_ref[pl.ds(pl.multiple_of(8 * (n_chunks + ch), 8), 1), :] + u_ref[n_chunks + ch]

    unroll = True if seq_chunks <= 4 else 16
    for sl in range(seqs):
        first = sl * seq_chunks
        if has_s0:
            st0_f = s0f_ref[...].T
            st0_b = s0b_ref[...].T
        else:
            st0_f = jnp.zeros((GLA_DV, dk), F32)
            st0_b = st0_f
        st_f = lax.fori_loop(first, first + seq_chunks, fwd_chunk, st0_f, unroll=unroll)
        st_b = lax.fori_loop(0, seq_chunks, functools.partial(bwd_chunk, last=first + seq_chunks - 1), st0_b,
                             unroll=unroll)
        if emit_state:
            st_ref[0, sl] = st_f.T
            st_ref[1, sl] = st_b.T


def _chunk_prefix_matrix():
    i = np.arange(P1_BLK)
    same = (i[:, None] >> CHUNK_SHIFT) == (i[None, :] >> CHUNK_SHIFT)
    return jnp.asarray(np.logical_and(same, i[None, :] <= i[:, None]).astype(np.float32), BF16)


def _gla_call(z, wgate, bgate, gnorm, layer, nseq, seq_len, s0=None, state_buf=None):
    m = nseq * seq_len
    tl = _chunk_prefix_matrix()
    has_s0 = s0 is not None
    emit_state = state_buf is not None
    seqs = 1
    if not has_s0:
        seqs = max(d for d in range(1, max(1, GLA_STEP_ROWS // seq_len) + 1) if nseq % d == 0)
    rows = seqs * seq_len
    kw, vw = GLA_DK, GLA_DV
    in_specs = [
        pl.BlockSpec((rows, kw), lambda s, h: (s, Z_Q // kw + h)),
        pl.BlockSpec((rows, kw), lambda s, h: (s, Z_K // kw + h)),
        pl.BlockSpec((rows, vw), lambda s, h: (s, Z_V // vw + h)),
        pl.BlockSpec((rows, vw), lambda s, h: (s, Z_G // vw + h)),
        pl.BlockSpec((rows, Z_GATE_PAD), lambda s, h: (s, Z_GATE // Z_GATE_PAD)),
        pl.BlockSpec((None, Z_GATE_PAD, 2 * kw), lambda s, h: (layer, 0, h)),
        pl.BlockSpec((None, 1, 2 * kw), lambda s, h: (layer, 0, h)),
        pl.BlockSpec((None, 1, vw), lambda s, h: (layer, 0, h)),
        pl.BlockSpec((P1_BLK, P1_BLK), lambda s, h: (0, 0)),
    ]
    args = [z, z, z, z, z, wgate, bgate, gnorm, tl]
    if has_s0:
        in_specs += [
            pl.BlockSpec((None, None, None, None, kw, vw), lambda s, h: (layer, 0, s, h, 0, 0)),
            pl.BlockSpec((None, None, None, None, kw, vw), lambda s, h: (layer, 1, s, h, 0, 0)),
        ]
        args += [s0, s0]
    aliases = {}
    if emit_state:
        aliases = {len(args): 1}
        in_specs.append(pl.BlockSpec(memory_space=pl.ANY))
        args.append(state_buf)
    out_specs = [pl.BlockSpec((rows, vw), lambda s, h: (s, h))]
    out_shape = [jax.ShapeDtypeStruct((m, GLA_WIDTH), BF16)]
    if emit_state:
        out_specs.append(pl.BlockSpec((None, 2, seqs, None, kw, vw), lambda s, h: (layer, 0, s, h, 0, 0)))
        out_shape.append(jax.ShapeDtypeStruct((DEPTH, 2, nseq, GLA_HEADS, kw, vw), F32))
    n_chunks = rows // GLA_CHUNK
    scratch = [
        pltpu.VMEM((rows, kw), BF16),
        pltpu.VMEM((rows, kw), BF16),
        pltpu.VMEM((rows, vw), F32),
        pltpu.VMEM((2 * n_chunks, vw, kw), F32),
        pltpu.VMEM((2 * n_chunks * 8, kw), F32),
    ]
    return pl.pallas_call(
        functools.partial(_gla_kernel, seq_len=seq_len, seqs=seqs, has_s0=has_s0, emit_state=emit_state),
        grid=(nseq // seqs, GLA_HEADS),
        in_specs=in_specs,
        out_specs=out_specs,
        out_shape=out_shape,
        scratch_shapes=scratch,
        input_output_aliases=aliases,
        compiler_params=_cparams(("arbitrary", "arbitrary")),
        name="gla_state" if emit_state else "gla",
    )(*args)


def _window_count(idx, n, w):
    lo = jnp.maximum(idx - w // 2, 0)
    hi = jnp.minimum(idx - w // 2 + w, n)
    return hi - lo


def _pool_project(pooled, gi, w_ref, sc_ref, o_ref, rows):
    cols = slice(gi * POOL_GROUP_DIM, (gi + 1) * POOL_GROUP_DIM)
    y = _dot(pooled.astype(BF16), w_ref[gi]) * sc_ref[:, cols]
    o_ref[rows, cols] = y.astype(BF16)


def _shifted(x, pos, n, d):
    rolled = pltpu.roll(x, (-d) % x.shape[0], 0)
    return jnp.where(jnp.logical_and(pos + d >= 0, pos + d < n), rolled, 0.0)


def _window_sum(x, pos, n, w):
    fwd = bwd = x
    s = 1
    while s < w // 2:
        fwd = fwd + _shifted(fwd, pos, n, s)
        bwd = bwd + _shifted(bwd, pos, n, -s)
        s *= 2
    return fwd + _shifted(bwd, pos, n, -1)


def _pool1d_kernel(p_ref, w_ref, sc_ref, o_ref, *, seq_len):
    t = lax.broadcasted_iota(jnp.int32, (seq_len, 1), 0)
    for gi, w in enumerate(POOL_WINDOWS):
        x = p_ref[:, gi * POOL_GROUP_DIM:(gi + 1) * POOL_GROUP_DIM]
        cnt = _window_count(t, seq_len, w).astype(F32)
        _pool_project(_window_sum(x, t, seq_len, w) / cnt - x, gi, w_ref, sc_ref, o_ref, slice(None))


def _pool2d_kernel(p_ref, w_ref, sc_ref, o_ref, cs_ref, *, seq_len):
    n_rows = seq_len // GRID_W
    pad = (max(POOL_WINDOWS) // 2) * GRID_W
    n_blk = seq_len // POOL_BLK
    zeros = jnp.zeros((pad, POOL_GROUP_DIM), F32)
    cs_ref[pl.ds(0, pad), :] = zeros
    cs_ref[pl.ds(pad + seq_len, pad), :] = zeros
    local = lax.broadcasted_iota(jnp.int32, (POOL_BLK, 1), 0)
    col = local & (GRID_W - 1)
    for gi, w in enumerate(POOL_WINDOWS):
        cols = slice(gi * POOL_GROUP_DIM, (gi + 1) * POOL_GROUP_DIM)

        def col_sums(i, carry, w=w, cols=cols):
            r = pl.multiple_of(i * POOL_BLK, POOL_BLK)
            acc = _window_sum(p_ref[pl.ds(r, POOL_BLK), cols], col, GRID_W, w)
            cs_ref[pl.ds(pl.multiple_of(r + pad, GRID_W), POOL_BLK), :] = acc
            return carry

        lax.fori_loop(0, n_blk, col_sums, 0)

        def row_sums(i, carry, w=w, cols=cols, gi=gi):
            r = pl.multiple_of(i * POOL_BLK, POOL_BLK)
            acc = jnp.zeros((POOL_BLK, POOL_GROUP_DIM), F32)
            for j in range(w):
                off = pad + (j - w // 2) * GRID_W
                acc = acc + cs_ref[pl.ds(pl.multiple_of(r + off, GRID_W), POOL_BLK), :]
            grow = (r + local) >> GRID_SHIFT
            cnt = (_window_count(grow, n_rows, w) * _window_count(col, GRID_W, w)).astype(F32)
            x = p_ref[pl.ds(r, POOL_BLK), cols]
            _pool_project(acc / cnt - x, gi, w_ref, sc_ref, o_ref, pl.ds(r, POOL_BLK))
            return carry

        lax.fori_loop(0, n_blk, row_sums, 0, unroll=2)


def _pool_call(z, w_pool, pool_scale, layer, nseq, seq_len, on_grid):
    m = nseq * seq_len
    if on_grid:
        body = functools.partial(_pool2d_kernel, seq_len=seq_len)
        pad = (max(POOL_WINDOWS) // 2) * GRID_W
        scratch = [pltpu.VMEM((seq_len + 2 * pad, POOL_GROUP_DIM), F32)]
    else:
        body = functools.partial(_pool1d_kernel, seq_len=seq_len)
        scratch = []
    n_groups = len(POOL_WINDOWS)
    return pl.pallas_call(
        body,
        grid=(nseq,),
        in_specs=[
            pl.BlockSpec((seq_len, POOL_WIDTH), lambda s: (s, Z_P // POOL_WIDTH)),
            pl.BlockSpec((None, n_groups, POOL_GROUP_DIM, POOL_GROUP_DIM), lambda s: (layer, 0, 0, 0)),
            _layer_vec_spec(layer, POOL_WIDTH),
        ],
        out_specs=pl.BlockSpec((seq_len, POOL_WIDTH), lambda s: (s, 0)),
        out_shape=jax.ShapeDtypeStruct((m, POOL_WIDTH), BF16),
        scratch_shapes=scratch,
        compiler_params=_cparams(("arbitrary",)),
        name="pool2d" if on_grid else "pool1d",
    )(z, w_pool, pool_scale)


def _mm_out_ln_kernel(mo_ref, mp_ref, w_ref, x_ref, g1_ref, sh2_ref, sc2_ref, lg_ref, lb_ref,
                      x1_ref, h2_ref):
    for s in range(x_ref.shape[0] // LN_SUB):
        rows = pl.ds(s * LN_SUB, LN_SUB)
        m = _dot(mo_ref[rows, :], w_ref[:GLA_WIDTH, :]) + _dot(mp_ref[rows, :], w_ref[GLA_WIDTH:, :])
        x1 = _layer_norm(ALPHA * x_ref[rows, :] + g1_ref[...] * m, lg_ref[...], lb_ref[...])
        x1_ref[rows, :] = x1
        h2_ref[rows, :] = (x1 * (1.0 + sc2_ref[...]) + sh2_ref[...]).astype(BF16)


def _mm_out_ln_call(mo, mp, w_out, x, mod4, rows_per_mod, ln_g, ln_b, layer):
    m = x.shape[0]
    tm = min(512, m)
    row = pl.BlockSpec((tm, D_MODEL), lambda i: (i, 0))
    half = pl.BlockSpec((tm, GLA_WIDTH), lambda i: (i, 0))
    w_spec = pl.BlockSpec((None, D_MODEL, D_MODEL), lambda i: (layer, 0, 0), pipeline_mode=pl.Buffered(1))
    mod = lambda part: _mod_spec(layer, part, rows_per_mod, tm)
    return pl.pallas_call(
        _mm_out_ln_kernel,
        grid=(m // tm,),
        in_specs=[half, half, w_spec, row, mod(2), mod(3), mod(4),
                  _layer_vec_spec(layer, D_MODEL), _layer_vec_spec(layer, D_MODEL)],
        out_specs=[row, row],
        out_shape=[jax.ShapeDtypeStruct((m, D_MODEL), F32), jax.ShapeDtypeStruct((m, D_MODEL), BF16)],
        compiler_params=_cparams(("arbitrary",)),
        name="mm_out_ln",
    )(mo, mp, w_out, x, mod4, mod4, mod4, ln_g, ln_b)


def _mm_gu_kernel(h_ref, wg_ref, wu_ref, o_ref):
    h = h_ref[...]
    for c in range(o_ref.shape[1] // GU_SUB):
        cols = pl.ds(c * GU_SUB, GU_SUB)
        gate = _dot(h, wg_ref[:, cols].astype(BF16))
        up = _dot(h, wu_ref[:, cols].astype(BF16))
        o_ref[:, cols] = (jax.nn.silu(gate) * up).astype(BF16)


def _mm_gu_call(h, w_gu, layer):
    m = h.shape[0]
    tm = min(1024, m)
    tn = 512
    nj = D_FF // tn
    return pl.pallas_call(
        _mm_gu_kernel,
        grid=(nj, m // tm),
        in_specs=[
            pl.BlockSpec((tm, D_MODEL), lambda j, i: (i, 0)),
            pl.BlockSpec((None, D_MODEL, tn), lambda j, i: (layer, 0, j)),
            pl.BlockSpec((None, D_MODEL, tn), lambda j, i: (layer, 0, nj + j)),
        ],
        out_specs=pl.BlockSpec((tm, tn), lambda j, i: (i, j)),
        out_shape=jax.ShapeDtypeStruct((m, D_FF), BF16),
        compiler_params=_cparams(("arbitrary", "arbitrary")),
        name="mm_gu",
    )(h, w_gu, w_gu)


def _mm_down_kernel(*refs, final_ln):
    a_ref, w_ref, x1_ref, g2_ref = refs[:4]
    if final_ln:
        lg_ref, lb_ref, y_ref = refs[4:]
    else:
        (y_ref,) = refs[4:]
    for s in range(y_ref.shape[0] // LN_SUB):
        rows = pl.ds(s * LN_SUB, LN_SUB)
        y = ALPHA * x1_ref[rows, :] + g2_ref[...] * _dot(a_ref[rows, :], w_ref[...])
        y_ref[rows, :] = _layer_norm(y, lg_ref[...], lb_ref[...]) if final_ln else y


def _mm_down_call(act, w_down, x1, mod4, rows_per_mod, ln_g, ln_b, layer):
    m = x1.shape[0]
    tm = min(512, m)
    final_ln = layer + 1 == DEPTH
    row = pl.BlockSpec((tm, D_MODEL), lambda i: (i, 0))
    in_specs = [
        pl.BlockSpec((tm, D_FF), lambda i: (i, 0)),
        pl.BlockSpec((None, D_FF, D_MODEL), lambda i: (layer, 0, 0), pipeline_mode=pl.Buffered(1)),
        row,
        _mod_spec(layer, 5, rows_per_mod, tm),
    ]
    args = [act, w_down, x1, mod4]
    if final_ln:
        in_specs += [_layer_vec_spec(layer, D_MODEL), _layer_vec_spec(layer, D_MODEL)]
        args += [ln_g, ln_b]
    return pl.pallas_call(
        functools.partial(_mm_down_kernel, final_ln=final_ln),
        grid=(m // tm,),
        in_specs=in_specs,
        out_specs=row,
        out_shape=jax.ShapeDtypeStruct((m, D_MODEL), F32),
        compiler_params=_cparams(("arbitrary",)),
        name="mm_down_ln" if final_ln else "mm_down",
    )(*args)


def _prep_weights(w_in, w_a2, b_a, w_pool, w_out, w_down):
    depth = w_in.shape[0]
    gate_pad = jnp.zeros((depth, D_MODEL, Z_GATE_PAD - 2 * GLA_GATE_RANK), w_in.dtype)
    w_in_r = jnp.concatenate(
        [w_in[:, :, :W_AF], w_in[:, :, W_P:], w_in[:, :, W_AF:W_P], gate_pad], axis=-1).astype(BF16)
    wgate = jnp.zeros((depth, Z_GATE_PAD, GLA_HEADS, 2, GLA_DK), F32)
    per_head = lambda t: t.reshape(depth, GLA_GATE_RANK, GLA_HEADS, GLA_DK)
    wgate = wgate.at[:, :GLA_GATE_RANK, :, 0, :].set(per_head(w_a2[:, 0]))
    wgate = wgate.at[:, GLA_GATE_RANK:2 * GLA_GATE_RANK, :, 1, :].set(per_head(w_a2[:, 1]))
    wgate = wgate.reshape(depth, Z_GATE_PAD, 2 * GLA_KEY_WIDTH)
    bgate = b_a.reshape(depth, 2, GLA_HEADS, GLA_DK).transpose(0, 2, 1, 3).reshape(depth, 1, 2 * GLA_KEY_WIDTH)
    return (w_in_r, wgate.astype(BF16), bgate, w_pool.astype(BF16), w_out.astype(BF16),
            w_down.astype(BF16))


def kernel(x_prompt, x_sample, state_gla, c, c_ctx, ln_in_g, ln_in_b, w_ada, b_ada, w_in, w_a2, b_a,
           gla_norm_g, w_pool, pool_scale, w_out, ln1_g, ln1_b, w_gu, w_down, ln2_g, ln2_b):
    nb_p, len_p, _ = x_prompt.shape
    nb_s, len_s, _ = x_sample.shape
    w_in_r, wgate, bgate, w_pool_b, w_out_b, w_down_b = _prep_weights(
        w_in, w_a2, b_a, w_pool, w_out, w_down)
    gnorm = gla_norm_g.reshape(DEPTH, 1, GLA_WIDTH)
    pscale = pool_scale.reshape(DEPTH, 1, POOL_WIDTH)
    ln1 = (ln1_g.reshape(DEPTH, 1, D_MODEL), ln1_b.reshape(DEPTH, 1, D_MODEL))
    ln2 = (ln2_g.reshape(DEPTH, 1, D_MODEL), ln2_b.reshape(DEPTH, 1, D_MODEL))

    n_mod = -(-(1 + nb_s) // 8) * 8
    c_all = jnp.concatenate(
        [c_ctx[None, :], c, jnp.zeros((n_mod - 1 - nb_s, D_MODEL), c.dtype)], axis=0)
    mod4 = _ada_call(c_all, w_ada, b_ada).reshape(DEPTH, n_mod, 1, 6 * D_MODEL)

    streams = [
        (x_prompt.reshape(nb_p * len_p, D_MODEL), None, nb_p, len_p, False),
        (x_sample.reshape(nb_s * len_s, D_MODEL), len_s, nb_s, len_s, True),
    ]
    results = []
    state_buf = jnp.zeros((DEPTH, 2, nb_p, GLA_HEADS, GLA_DK, GLA_DV), F32)
    ln_in = (ln_in_g.reshape(1, 1, D_MODEL), ln_in_b.reshape(1, 1, D_MODEL))
    for y, rows_per_mod, nseq, seq_len, on_grid in streams:
        for l in range(DEPTH):
            ln_g, ln_b, ln_layer = (ln_in[0], ln_in[1], 0) if l == 0 else (ln2[0], ln2[1], l - 1)
            z, x = _ln_mm_in_call(y, ln_g, ln_b, ln_layer, mod4, rows_per_mod, w_in_r, l)
            if on_grid:
                (mo,) = _gla_call(z, wgate, bgate, gnorm, l, nseq, seq_len, s0=state_gla)
            else:
                mo, state_buf = _gla_call(z, wgate, bgate, gnorm, l, nseq, seq_len, state_buf=state_buf)
            mp = _pool_call(z, w_pool_b, pscale, l, nseq, seq_len, on_grid)
            x1, h2 = _mm_out_ln_call(mo, mp, w_out_b, x, mod4, rows_per_mod, ln1[0], ln1[1], l)
            act = _mm_gu_call(h2, w_gu, l)
            y = _mm_down_call(act, w_down_b, x1, mod4, rows_per_mod, ln2[0], ln2[1], l)
        results.append(y)
    y_prompt = results[0].reshape(nb_p, len_p, D_MODEL)
    y_sample = results[1].reshape(nb_s, len_s, D_MODEL)
    return (y_prompt, y_sample, state_buf.astype(x_prompt.dtype))
```

```python
import functools

import numpy as np
import jax
import jax.numpy as jnp
from jax import lax
from jax.experimental import pallas as pl
from jax.experimental.pallas import tpu as pltpu

F32 = jnp.float32
BF16 = jnp.bfloat16

D_MODEL = 2048
DEPTH = 4
GLA_HEADS = 4
GLA_DV = 256
GLA_DK = 128
GLA_KEY_WIDTH = GLA_HEADS * GLA_DK
GLA_WIDTH = GLA_HEADS * GLA_DV
POOL_WIDTH = 1024
GLA_GATE_RANK = 16
GLA_GATE_TEMP = 16.0
GLA_CHUNK = 64
POOL_WINDOWS = (2, 4, 8, 16)
POOL_GROUP_DIM = 256
GRID_W = 64
D_FF = 5632
ALPHA = (2 * DEPTH) ** 0.25
LN_EPS = 1e-5
Q_SCALE = GLA_DK ** -0.5

Z_Q = 0
Z_K = Z_Q + GLA_KEY_WIDTH
Z_V = Z_K + GLA_KEY_WIDTH
Z_G = Z_V + GLA_WIDTH
Z_P = Z_G + GLA_WIDTH
Z_GATE = Z_P + POOL_WIDTH
Z_GATE_PAD = 128
Z_WIDTH = Z_GATE + Z_GATE_PAD
W_AF = 2 * GLA_KEY_WIDTH + 2 * GLA_WIDTH
W_P = W_AF + 2 * GLA_GATE_RANK

P1_BLK = 256
GLA_STEP_ROWS = 2048
POOL_BLK = 256
GU_SUB = 256
LN_SUB = 128
CHUNK_SHIFT = GLA_CHUNK.bit_length() - 1
GRID_SHIFT = GRID_W.bit_length() - 1
V7X_VMEM_LIMIT_BYTES = 58 * 2 ** 20


def _log_sigmoid(x):
    return jnp.minimum(x, 0.0) - jnp.log1p(jnp.exp(-jnp.abs(x)))


def _dot(a, b):
    return jnp.dot(a, b, preferred_element_type=F32)


def _dot_nt(a, b):
    return lax.dot_general(a, b, (((1,), (1,)), ((), ())), preferred_element_type=F32)


def _band_sum(band, x):
    hi = x.astype(BF16)
    r1 = x - hi.astype(F32)
    mid = r1.astype(BF16)
    lo = (r1 - mid.astype(F32)).astype(BF16)
    return _dot(band, hi) + _dot(band, mid) + _dot(band, lo)


def _cparams(semantics):
    return pltpu.CompilerParams(dimension_semantics=semantics, vmem_limit_bytes=V7X_VMEM_LIMIT_BYTES)


def _layer_norm(y, g, b):
    mu = jnp.mean(y, axis=-1, keepdims=True)
    yc = y - mu
    var = jnp.mean(yc * yc, axis=-1, keepdims=True)
    return yc * lax.rsqrt(var + LN_EPS) * g + b


def _mod_spec(layer, part, rows_per_mod, tm):
    if rows_per_mod is None:
        return pl.BlockSpec((None, None, 1, D_MODEL), lambda i, *_: (layer, 0, 0, part))
    return pl.BlockSpec((None, None, 1, D_MODEL),
                        lambda i, *_: (layer, 1 + (i * tm) // rows_per_mod, 0, part))


def _layer_vec_spec(layer, width):
    return pl.BlockSpec((None, 1, width), lambda *_: (layer, 0, 0))


def _ada_kernel(c_ref, w_ref, b_ref, o_ref):
    c = c_ref[...]
    s = jax.nn.silu(c).astype(BF16)
    o_ref[...] = _dot(s, w_ref[...].astype(BF16)) + b_ref[...]


def _ada_call(c_all, w_ada, b_ada):
    rows = c_all.shape[0]
    tn = 1024
    nj = (6 * D_MODEL) // tn
    return pl.pallas_call(
        _ada_kernel,
        grid=(DEPTH, nj),
        in_specs=[
            pl.BlockSpec((rows, D_MODEL), lambda l, j: (0, 0)),
            pl.BlockSpec((None, D_MODEL, tn), lambda l, j: (l, 0, j)),
            pl.BlockSpec((None, 1, tn), lambda l, j: (l, 0, j)),
        ],
        out_specs=pl.BlockSpec((None, rows, tn), lambda l, j: (l, 0, j)),
        out_shape=jax.ShapeDtypeStruct((DEPTH, rows, 6 * D_MODEL), F32),
        compiler_params=_cparams(("arbitrary", "arbitrary")),
        name="ada_mod",
    )(c_all, w_ada, b_ada.reshape(DEPTH, 1, 6 * D_MODEL))


def _ln_mm_in_kernel(y_ref, lg_ref, lb_ref, sh_ref, sc_ref, w_ref, z_ref, x_ref):
    for s in range(y_ref.shape[0] // LN_SUB):
        rows = pl.ds(s * LN_SUB, LN_SUB)
        x = _layer_norm(y_ref[rows, :], lg_ref[...], lb_ref[...])
        x_ref[rows, :] = x
        h = (x * (1.0 + sc_ref[...]) + sh_ref[...]).astype(BF16)
        z_ref[rows, :] = _dot(h, w_ref[...])


def _ln_mm_in_call(y, ln_g, ln_b, ln_layer, mod4, rows_per_mod, w, layer):
    m = y.shape[0]
    tm = min(512, m)
    row = pl.BlockSpec((tm, D_MODEL), lambda i: (i, 0))
    return pl.pallas_call(
        _ln_mm_in_kernel,
        grid=(m // tm,),
        in_specs=[
            row, _layer_vec_spec(ln_layer, D_MODEL), _layer_vec_spec(ln_layer, D_MODEL),
            _mod_spec(layer, 0, rows_per_mod, tm), _mod_spec(layer, 1, rows_per_mod, tm),
            pl.BlockSpec((None, D_MODEL, Z_WIDTH), lambda i: (layer, 0, 0), pipeline_mode=pl.Buffered(1)),
        ],
        out_specs=[pl.BlockSpec((tm, Z_WIDTH), lambda i: (i, 0)), row],
        out_shape=[jax.ShapeDtypeStruct((m, Z_WIDTH), F32), jax.ShapeDtypeStruct((m, D_MODEL), F32)],
        compiler_params=_cparams(("arbitrary",)),
        name="ln_mm_in",
    )(y, ln_g, ln_b, mod4, mod4, w)


def _gla_kernel(*refs, seq_len, seqs, has_s0, emit_state):
    (q_ref, k_ref, v_ref, g_ref, zg_ref, wg_ref, bg_ref, gn_ref, tl_ref) = refs[:9]
    pos = 9
    if has_s0:
        s0f_ref, s0b_ref = refs[pos:pos + 2]
        pos += 2
    if emit_state:
        pos += 1
    mo_ref = refs[pos]
    pos += 1
    if emit_state:
        st_ref = refs[pos]
        pos += 1
    qef, qeb, o_ref, u_ref, dec_ref = refs[pos:]

    n_blk = seqs * seq_len // P1_BLK
    n_chunks = seqs * seq_len // GLA_CHUNK
    seq_chunks = seq_len // GLA_CHUNK
    c = GLA_CHUNK
    dk = GLA_DK

    brow = lax.broadcasted_iota(jnp.int32, (P1_BLK, P1_BLK), 0)
    bcol = lax.broadcasted_iota(jnp.int32, (P1_BLK, P1_BLK), 1)
    same_chunk = (brow >> CHUNK_SHIFT) == (bcol >> CHUNK_SHIFT)
    mask_f = jnp.logical_and(same_chunk, bcol <= brow)
    mask_b = jnp.logical_and(same_chunk, bcol >= brow)
    row_chunk = lax.broadcasted_iota(jnp.int32, (P1_BLK, dk), 0) >> CHUNK_SHIFT

    def phase_a(i, carry):
        r = pl.multiple_of(i * P1_BLK, P1_BLK)
        rows = pl.ds(r, P1_BLK)
        zg = zg_ref[rows, :].astype(BF16)
        la = _log_sigmoid(_dot(zg, wg_ref[...]) + bg_ref[...]) * (1.0 / GLA_GATE_TEMP)
        pre = _band_sum(tl_ref[...], la)
        n_sub = P1_BLK // c
        tot = jnp.concatenate(
            [jnp.broadcast_to(pre[(s + 1) * c - 1:(s + 1) * c, :], (c, 2 * dk)) for s in range(n_sub)], axis=0)
        b_f = pre[:, :dk]
        tot_f = tot[:, :dk]
        tot_b = tot[:, dk:]
        b_b = tot_b - pre[:, dk:] + la[:, dk:]
        q = q_ref[rows, :] * Q_SCALE
        k = k_ref[rows, :]
        qe_f = (q * jnp.exp(b_f)).astype(BF16)
        qe_b = (q * jnp.exp(b_b)).astype(BF16)
        qef[rows, :] = qe_f
        qeb[rows, :] = qe_b
        ke_f = (k * jnp.exp(-b_f)).astype(BF16)
        ke_b = (k * jnp.exp(-b_b)).astype(BF16)
        a = (jnp.where(mask_f, _dot_nt(qe_f, ke_f), 0.0) + jnp.where(mask_b, _dot_nt(qe_b, ke_b), 0.0))
        v = v_ref[rows, :]
        o_ref[rows, :] = _dot(a.astype(BF16), v.astype(BF16))
        kd_f = k * jnp.exp(tot_f - b_f)
        kd_b = k * jnp.exp(tot_b - b_b)
        kd_cols = [jnp.where(row_chunk == s, kd, 0.0).astype(BF16) for kd in (kd_f, kd_b) for s in range(n_sub)]
        u = _dot(v.T.astype(BF16), jnp.concatenate(kd_cols, axis=-1))
        for s in range(n_sub):
            ch = n_sub * i + s
            u_ref[ch] = u[:, s * dk:(s + 1) * dk]
            u_ref[n_chunks + ch] = u[:, (n_sub + s) * dk:(n_sub + s + 1) * dk]
            dec = jnp.exp(tot[s * c:s * c + 8, :])
            dec_ref[pl.ds(pl.multiple_of(8 * ch, 8), 8), :] = dec[:, :dk]
            dec_ref[pl.ds(pl.multiple_of(8 * (n_chunks + ch), 8), 8), :] = dec[:, dk:]
        return carry

    lax.fori_loop(0, n_blk, phase_a, 0, unroll=max(u for u in (1, 2, 4, 8) if n_blk % u == 0))

    def fwd_chunk(ch, st):
        rows = pl.ds(pl.multiple_of(ch * c, c), c)
        o_ref[rows, :] += _dot_nt(qef[rows, :], st.astype(BF16))
        return st * dec_ref[pl.ds(pl.multiple_of(8 * ch, 8), 1), :] + u_ref[ch]

    def bwd_chunk(j, st, last):
        ch = last - j
        rows = pl.ds(pl.multiple_of(ch * c, c), c)
        o = o_ref[rows, :] + _dot_nt(qeb[rows, :], st.astype(BF16))
        ms = jnp.mean(o * o, axis=-1, keepdims=True)
        y = o * lax.rsqrt(ms + LN_EPS) * gn_ref[...]
        mo_ref[rows, :] = (y * jax.nn.silu(g_ref[rows, :])).astype(BF16)
        return st * dec_ref[pl.ds(pl.multiple_of(8 * (n_chunks + ch), 8), 1), :] + u_ref[n_chunks + ch]

    unroll = True if seq_chunks <= 4 else 16
    for sl in range(seqs):
        first = sl * seq_chunks
        if has_s0:
            st0_f = s0f_ref[...].T
            st0_b = s0b_ref[...].T
        else:
            st0_f = jnp.zeros((GLA_DV, dk), F32)
            st0_b = st0_f
        st_f = lax.fori_loop(first, first + seq_chunks, fwd_chunk, st0_f, unroll=unroll)
        st_b = lax.fori_loop(0, seq_chunks, functools.partial(bwd_chunk, last=first + seq_chunks - 1), st0_b,
                             unroll=unroll)
        if emit_state:
            st_ref[0, sl] = st_f.T
            st_ref[1, sl] = st_b.T


def _chunk_prefix_matrix():
    i = np.arange(P1_BLK)
    same = (i[:, None] >> CHUNK_SHIFT) == (i[None, :] >> CHUNK_SHIFT)
    return jnp.asarray(np.logical_and(same, i[None, :] <= i[:, None]).astype(np.float32), BF16)


def _gla_call(z, wgate, bgate, gnorm, layer, nseq, seq_len, s0=None, state_buf=None):
    m = nseq * seq_len
    tl = _chunk_prefix_matrix()
    has_s0 = s0 is not None
    emit_state = state_buf is not None
    seqs = 1
    if not has_s0:
        seqs = max(d for d in range(1, max(1, GLA_STEP_ROWS // seq_len) + 1) if nseq % d == 0)
    rows = seqs * seq_len
    kw, vw = GLA_DK, GLA_DV
    in_specs = [
        pl.BlockSpec((rows, kw), lambda s, h: (s, Z_Q // kw + h)),
        pl.BlockSpec((rows, kw), lambda s, h: (s, Z_K // kw + h)),
        pl.BlockSpec((rows, vw), lambda s, h: (s, Z_V // vw + h)),
        pl.BlockSpec((rows, vw), lambda s, h: (s, Z_G // vw + h)),
        pl.BlockSpec((rows, Z_GATE_PAD), lambda s, h: (s, Z_GATE // Z_GATE_PAD)),
        pl.BlockSpec((None, Z_GATE_PAD, 2 * kw), lambda s, h: (layer, 0, h)),
        pl.BlockSpec((None, 1, 2 * kw), lambda s, h: (layer, 0, h)),
        pl.BlockSpec((None, 1, vw), lambda s, h: (layer, 0, h)),
        pl.BlockSpec((P1_BLK, P1_BLK), lambda s, h: (0, 0)),
    ]
    args = [z, z, z, z, z, wgate, bgate, gnorm, tl]
    if has_s0:
        in_specs += [
            pl.BlockSpec((None, None, None, None, kw, vw), lambda s, h: (layer, 0, s, h, 0, 0)),
            pl.BlockSpec((None, None, None, None, kw, vw), lambda s, h: (layer, 1, s, h, 0, 0)),
        ]
        args += [s0, s0]
    aliases = {}
    if emit_state:
        aliases = {len(args): 1}
        in_specs.append(pl.BlockSpec(memory_space=pl.ANY))
        args.append(state_buf)
    out_specs = [pl.BlockSpec((rows, vw), lambda s, h: (s, h))]
    out_shape = [jax.ShapeDtypeStruct((m, GLA_WIDTH), BF16)]
    if emit_state:
        out_specs.append(pl.BlockSpec((None, 2, seqs, None, kw, vw), lambda s, h: (layer, 0, s, h, 0, 0)))
        out_shape.append(jax.ShapeDtypeStruct((DEPTH, 2, nseq, GLA_HEADS, kw, vw), F32))
    n_chunks = rows // GLA_CHUNK
    scratch = [
        pltpu.VMEM((rows, kw), BF16),
        pltpu.VMEM((rows, kw), BF16),
        pltpu.VMEM((rows, vw), F32),
        pltpu.VMEM((2 * n_chunks, vw, kw), F32),
        pltpu.VMEM((2 * n_chunks * 8, kw), F32),
    ]
    return pl.pallas_call(
        functools.partial(_gla_kernel, seq_len=seq_len, seqs=seqs, has_s0=has_s0, emit_state=emit_state),
        grid=(nseq // seqs, GLA_HEADS),
        in_specs=in_specs,
        out_specs=out_specs,
        out_shape=out_shape,
        scratch_shapes=scratch,
        input_output_aliases=aliases,
        compiler_params=_cparams(("arbitrary", "arbitrary")),
        name="gla_state" if emit_state else "gla",
    )(*args)


def _window_count(idx, n, w):
    lo = jnp.maximum(idx - w // 2, 0)
    hi = jnp.minimum(idx - w // 2 + w, n)
    return hi - lo


def _pool_project(pooled, gi, w_ref, sc_ref, o_ref, rows):
    cols = slice(gi * POOL_GROUP_DIM, (gi + 1) * POOL_GROUP_DIM)
    y = _dot(pooled.astype(BF16), w_ref[gi]) * sc_ref[:, cols]
    o_ref[rows, cols] = y.astype(BF16)


def _shifted(x, pos, n, d):
    rolled = pltpu.roll(x, (-d) % x.shape[0], 0)
    return jnp.where(jnp.logical_and(pos + d >= 0, pos + d < n), rolled, 0.0)


def _window_sum(x, pos, n, w):
    fwd = bwd = x
    s = 1
    while s < w // 2:
        fwd = fwd + _shifted(fwd, pos, n, s)
        bwd = bwd + _shifted(bwd, pos, n, -s)
        s *= 2
    return fwd + _shifted(bwd, pos, n, -1)


def _pool1d_kernel(p_ref, w_ref, sc_ref, o_ref, *, seq_len):
    t = lax.broadcasted_iota(jnp.int32, (seq_len, 1), 0)
    for gi, w in enumerate(POOL_WINDOWS):
        x = p_ref[:, gi * POOL_GROUP_DIM:(gi + 1) * POOL_GROUP_DIM]
        cnt = _window_count(t, seq_len, w).astype(F32)
        _pool_project(_window_sum(x, t, seq_len, w) / cnt - x, gi, w_ref, sc_ref, o_ref, slice(None))


def _pool2d_kernel(p_ref, w_ref, sc_ref, o_ref, cs_ref, *, seq_len):
    n_rows = seq_len // GRID_W
    pad = (max(POOL_WINDOWS) // 2) * GRID_W
    n_blk = seq_len // POOL_BLK
    zeros = jnp.zeros((pad, POOL_GROUP_DIM), F32)
    cs_ref[pl.ds(0, pad), :] = zeros
    cs_ref[pl.ds(pad + seq_len, pad), :] = zeros
    local = lax.broadcasted_iota(jnp.int32, (POOL_BLK, 1), 0)
    col = local & (GRID_W - 1)
    for gi, w in enumerate(POOL_WINDOWS):
        cols = slice(gi * POOL_GROUP_DIM, (gi + 1) * POOL_GROUP_DIM)

        def col_sums(i, carry, w=w, cols=cols):
            r = pl.multiple_of(i * POOL_BLK, POOL_BLK)
            acc = _window_sum(p_ref[pl.ds(r, POOL_BLK), cols], col, GRID_W, w)
            cs_ref[pl.ds(pl.multiple_of(r + pad, GRID_W), POOL_BLK), :] = acc
            return carry

        lax.fori_loop(0, n_blk, col_sums, 0)

        def row_sums(i, carry, w=w, cols=cols, gi=gi):
            r = pl.multiple_of(i * POOL_BLK, POOL_BLK)
            acc = jnp.zeros((POOL_BLK, POOL_GROUP_DIM), F32)
            for j in range(w):
                off = pad + (j - w // 2) * GRID_W
                acc = acc + cs_ref[pl.ds(pl.multiple_of(r + off, GRID_W), POOL_BLK), :]
            grow = (r + local) >> GRID_SHIFT
            cnt = (_window_count(grow, n_rows, w) * _window_count(col, GRID_W, w)).astype(F32)
            x = p_ref[pl.ds(r, POOL_BLK), cols]
            _pool_project(acc / cnt - x, gi, w_ref, sc_ref, o_ref, pl.ds(r, POOL_BLK))
            return carry

        lax.fori_loop(0, n_blk, row_sums, 0, unroll=2)


def _pool_call(z, w_pool, pool_scale, layer, nseq, seq_len, on_grid):
    m = nseq * seq_len
    if on_grid:
        body = functools.partial(_pool2d_kernel, seq_len=seq_len)
        pad = (max(POOL_WINDOWS) // 2) * GRID_W
        scratch = [pltpu.VMEM((seq_len + 2 * pad, POOL_GROUP_DIM), F32)]
    else:
        body = functools.partial(_pool1d_kernel, seq_len=seq_len)
        scratch = []
    n_groups = len(POOL_WINDOWS)
    return pl.pallas_call(
        body,
        grid=(nseq,),
        in_specs=[
            pl.BlockSpec((seq_len, POOL_WIDTH), lambda s: (s, Z_P // POOL_WIDTH)),
            pl.BlockSpec((None, n_groups, POOL_GROUP_DIM, POOL_GROUP_DIM), lambda s: (layer, 0, 0, 0)),
            _layer_vec_spec(layer, POOL_WIDTH),
        ],
        out_specs=pl.BlockSpec((seq_len, POOL_WIDTH), lambda s: (s, 0)),
        out_shape=jax.ShapeDtypeStruct((m, POOL_WIDTH), BF16),
        scratch_shapes=scratch,
        compiler_params=_cparams(("arbitrary",)),
        name="pool2d" if on_grid else "pool1d",
    )(z, w_pool, pool_scale)


def _mm_out_ln_kernel(mo_ref, mp_ref, w_ref, x_ref, g1_ref, sh2_ref, sc2_ref, lg_ref, lb_ref,
                      x1_ref, h2_ref):
    for s in range(x_ref.shape[0] // LN_SUB):
        rows = pl.ds(s * LN_SUB, LN_SUB)
        m = _dot(mo_ref[rows, :], w_ref[:GLA_WIDTH, :]) + _dot(mp_ref[rows, :], w_ref[GLA_WIDTH:, :])
        x1 = _layer_norm(ALPHA * x_ref[rows, :] + g1_ref[...] * m, lg_ref[...], lb_ref[...])
        x1_ref[rows, :] = x1
        h2_ref[rows, :] = (x1 * (1.0 + sc2_ref[...]) + sh2_ref[...]).astype(BF16)


def _mm_out_ln_call(mo, mp, w_out, x, mod4, rows_per_mod, ln_g, ln_b, layer):
    m = x.shape[0]
    tm = min(512, m)
    row = pl.BlockSpec((tm, D_MODEL), lambda i: (i, 0))
    half = pl.BlockSpec((tm, GLA_WIDTH), lambda i: (i, 0))
    w_spec = pl.BlockSpec((None, D_MODEL, D_MODEL), lambda i: (layer, 0, 0), pipeline_mode=pl.Buffered(1))
    mod = lambda part: _mod_spec(layer, part, rows_per_mod, tm)
    return pl.pallas_call(
        _mm_out_ln_kernel,
        grid=(m // tm,),
        in_specs=[half, half, w_spec, row, mod(2), mod(3), mod(4),
                  _layer_vec_spec(layer, D_MODEL), _layer_vec_spec(layer, D_MODEL)],
        out_specs=[row, row],
        out_shape=[jax.ShapeDtypeStruct((m, D_MODEL), F32), jax.ShapeDtypeStruct((m, D_MODEL), BF16)],
        compiler_params=_cparams(("arbitrary",)),
        name="mm_out_ln",
    )(mo, mp, w_out, x, mod4, mod4, mod4, ln_g, ln_b)


def _mm_gu_kernel(h_ref, wg_ref, wu_ref, o_ref):
    h = h_ref[...]
    for c in range(o_ref.shape[1] // GU_SUB):
        cols = pl.ds(c * GU_SUB, GU_SUB)
        gate = _dot(h, wg_ref[:, cols].astype(BF16))
        up = _dot(h, wu_ref[:, cols].astype(BF16))
        o_ref[:, cols] = (jax.nn.silu(gate) * up).astype(BF16)


def _mm_gu_call(h, w_gu, layer):
    m = h.shape[0]
    tm = min(1024, m)
    tn = 512
    nj = D_FF // tn
    return pl.pallas_call(
        _mm_gu_kernel,
        grid=(nj, m // tm),
        in_specs=[
            pl.BlockSpec((tm, D_MODEL), lambda j, i: (i, 0)),
            pl.BlockSpec((None, D_MODEL, tn), lambda j, i: (layer, 0, j)),
            pl.BlockSpec((None, D_MODEL, tn), lambda j, i: (layer, 0, nj + j)),
        ],
        out_specs=pl.BlockSpec((tm, tn), lambda j, i: (i, j)),
        out_shape=jax.ShapeDtypeStruct((m, D_FF), BF16),
        compiler_params=_cparams(("arbitrary", "arbitrary")),
        name="mm_gu",
    )(h, w_gu, w_gu)


def _mm_down_kernel(*refs, final_ln):
    a_ref, w_ref, x1_ref, g2_ref = refs[:4]
    if final_ln:
        lg_ref, lb_ref, y_ref = refs[4:]
    else:
        (y_ref,) = refs[4:]
    for s in range(y_ref.shape[0] // LN_SUB):
        rows = pl.ds(s * LN_SUB, LN_SUB)
        y = ALPHA * x1_ref[rows, :] + g2_ref[...] * _dot(a_ref[rows, :], w_ref[...])
        y_ref[rows, :] = _layer_norm(y, lg_ref[...], lb_ref[...]) if final_ln else y


def _mm_down_call(act, w_down, x1, mod4, rows_per_mod, ln_g, ln_b, layer):
    m = x1.shape[0]
    tm = min(512, m)
    final_ln = layer + 1 == DEPTH
    row = pl.BlockSpec((tm, D_MODEL), lambda i: (i, 0))
    in_specs = [
        pl.BlockSpec((tm, D_FF), lambda i: (i, 0)),
        pl.BlockSpec((None, D_FF, D_MODEL), lambda i: (layer, 0, 0), pipeline_mode=pl.Buffered(1)),
        row,
        _mod_spec(layer, 5, rows_per_mod, tm),
    ]
    args = [act, w_down, x1, mod4]
    if final_ln:
        in_specs += [_layer_vec_spec(layer, D_MODEL), _layer_vec_spec(layer, D_MODEL)]
        args += [ln_g, ln_b]
    return pl.pallas_call(
        functools.partial(_mm_down_kernel, final_ln=final_ln),
        grid=(m // tm,),
        in_specs=in_specs,
        out_specs=row,
        out_shape=jax.ShapeDtypeStruct((m, D_MODEL), F32),
        compiler_params=_cparams(("arbitrary",)),
        name="mm_down_ln" if final_ln else "mm_down",
    )(*args)


def _prep_weights(w_in, w_a2, b_a, w_pool, w_out, w_down):
    depth = w_in.shape[0]
    gate_pad = jnp.zeros((depth, D_MODEL, Z_GATE_PAD - 2 * GLA_GATE_RANK), w_in.dtype)
    w_in_r = jnp.concatenate(
        [w_in[:, :, :W_AF], w_in[:, :, W_P:], w_in[:, :, W_AF:W_P], gate_pad], axis=-1).astype(BF16)
    wgate = jnp.zeros((depth, Z_GATE_PAD, GLA_HEADS, 2, GLA_DK), F32)
    per_head = lambda t: t.reshape(depth, GLA_GATE_RANK, GLA_HEADS, GLA_DK)
    wgate = wgate.at[:, :GLA_GATE_RANK, :, 0, :].set(per_head(w_a2[:, 0]))
    wgate = wgate.at[:, GLA_GATE_RANK:2 * GLA_GATE_RANK, :, 1, :].set(per_head(w_a2[:, 1]))
    wgate = wgate.reshape(depth, Z_GATE_PAD, 2 * GLA_KEY_WIDTH)
    bgate = b_a.reshape(depth, 2, GLA_HEADS, GLA_DK).transpose(0, 2, 1, 3).reshape(depth, 1, 2 * GLA_KEY_WIDTH)
    return (w_in_r, wgate.astype(BF16), bgate, w_pool.astype(BF16), w_out.astype(BF16),
            w_down.astype(BF16))


def kernel(x_prompt, x_sample, state_gla, c, c_ctx, ln_in_g, ln_in_b, w_ada, b_ada, w_in, w_a2, b_a,
           gla_norm_g, w_pool, pool_scale, w_out, ln1_g, ln1_b, w_gu, w_down, ln2_g, ln2_b):
    nb_p, len_p, _ = x_prompt.shape
    nb_s, len_s, _ = x_sample.shape
    w_in_r, wgate, bgate, w_pool_b, w_out_b, w_down_b = _prep_weights(
        w_in, w_a2, b_a, w_pool, w_out, w_down)
    gnorm = gla_norm_g.reshape(DEPTH, 1, GLA_WIDTH)
    pscale = pool_scale.reshape(DEPTH, 1, POOL_WIDTH)
    ln1 = (ln1_g.reshape(DEPTH, 1, D_MODEL), ln1_b.reshape(DEPTH, 1, D_MODEL))
    ln2 = (ln2_g.reshape(DEPTH, 1, D_MODEL), ln2_b.reshape(DEPTH, 1, D_MODEL))

    n_mod = -(-(1 + nb_s) // 8) * 8
    c_all = jnp.concatenate(
        [c_ctx[None, :], c, jnp.zeros((n_mod - 1 - nb_s, D_MODEL), c.dtype)], axis=0)
    mod4 = _ada_call(c_all, w_ada, b_ada).reshape(DEPTH, n_mod, 1, 6 * D_MODEL)

    streams = [
        (x_prompt.reshape(nb_p * len_p, D_MODEL), None, nb_p, len_p, False),
        (x_sample.reshape(nb_s * len_s, D_MODEL), len_s, nb_s, len_s, True),
    ]
    results = []
    state_buf = jnp.zeros((DEPTH, 2, nb_p, GLA_HEADS, GLA_DK, GLA_DV), F32)
    ln_in = (ln_in_g.reshape(1, 1, D_MODEL), ln_in_b.reshape(1, 1, D_MODEL))
    for y, rows_per_mod, nseq, seq_len, on_grid in streams:
        for l in range(DEPTH):
            ln_g, ln_b, ln_layer = (ln_in[0], ln_in[1], 0) if l == 0 else (ln2[0], ln2[1], l - 1)
            z, x = _ln_mm_in_call(y, ln_g, ln_b, ln_layer, mod4, rows_per_mod, w_in_r, l)
            if on_grid:
                (mo,) = _gla_call(z, wgate, bgate, gnorm, l, nseq, seq_len, s0=state_gla)
            else:
                mo, state_buf = _gla_call(z, wgate, bgate, gnorm, l, nseq, seq_len, state_buf=state_buf)
            mp = _pool_call(z, w_pool_b, pscale, l, nseq, seq_len, on_grid)
            x1, h2 = _mm_out_ln_call(mo, mp, w_out_b, x, mod4, rows_per_mod, ln1[0], ln1[1], l)
            act = _mm_gu_call(h2, w_gu, l)
            y = _mm_down_call(act, w_down_b, x1, mod4, rows_per_mod, ln2[0], ln2[1], l)
        results.append(y)
    y_prompt = results[0].reshape(nb_p, len_p, D_MODEL)
    y_sample = results[1].reshape(nb_s, len_s, D_MODEL)
    return (y_prompt, y_sample, state_buf.astype(x_prompt.dtype))
```

```python
import functools

import numpy as np
import jax
import jax.numpy as jnp
from jax import lax
from jax.experimental import pallas as pl
from jax.experimental.pallas import tpu as pltpu

F32 = jnp.float32
BF16 = jnp.bfloat16

D_MODEL = 2048
DEPTH = 4
GLA_HEADS = 4
GLA_DV = 256
GLA_DK = 128
GLA_KEY_WIDTH = GLA_HEADS * GLA_DK
GLA_WIDTH = GLA_HEADS * GLA_DV
POOL_WIDTH = 1024
GLA_GATE_RANK = 16
GLA_GATE_TEMP = 16.0
GLA_CHUNK = 64
POOL_WINDOWS = (2, 4, 8, 16)
POOL_GROUP_DIM = 256
GRID_W = 64
D_FF = 5632
ALPHA = (2 * DEPTH) ** 0.25
LN_EPS = 1e-5
Q_SCALE = GLA_DK ** -0.5

Z_Q = 0
Z_K = Z_Q + GLA_KEY_WIDTH
Z_V = Z_K + GLA_KEY_WIDTH
Z_G = Z_V + GLA_WIDTH
Z_P = Z_G + GLA_WIDTH
Z_GATE = Z_P + POOL_WIDTH
Z_GATE_PAD = 128
Z_WIDTH = Z_GATE + Z_GATE_PAD
W_AF = 2 * GLA_KEY_WIDTH + 2 * GLA_WIDTH
W_P = W_AF + 2 * GLA_GATE_RANK

P1_BLK = 256
GLA_STEP_ROWS = 2048
POOL_BLK = 256
POOL_STEP_ROWS = 1024
GU_SUB = 256
LN_SUB = 128
CHUNK_SHIFT = GLA_CHUNK.bit_length() - 1
GRID_SHIFT = GRID_W.bit_length() - 1
V7X_VMEM_LIMIT_BYTES = 58 * 2 ** 20


def _log_sigmoid(x):
    return jnp.minimum(x, 0.0) - jnp.log1p(jnp.exp(-jnp.abs(x)))


def _dot(a, b):
    return jnp.dot(a, b, preferred_element_type=F32)


def _dot_nt(a, b):
    return lax.dot_general(a, b, (((1,), (1,)), ((), ())), preferred_element_type=F32)


def _band_sum(band, x):
    hi = x.astype(BF16)
    r1 = x - hi.astype(F32)
    mid = r1.astype(BF16)
    lo = (r1 - mid.astype(F32)).astype(BF16)
    return _dot(band, hi) + _dot(band, mid) + _dot(band, lo)


def _cparams(semantics):
    return pltpu.CompilerParams(dimension_semantics=semantics, vmem_limit_bytes=V7X_VMEM_LIMIT_BYTES)


def _layer_norm(y, g, b):
    mu = jnp.mean(y, axis=-1, keepdims=True)
    yc = y - mu
    var = jnp.mean(yc * yc, axis=-1, keepdims=True)
    return yc * lax.rsqrt(var + LN_EPS) * g + b


def _mod_spec(layer, part, rows_per_mod, tm):
    if rows_per_mod is None:
        return pl.BlockSpec((None, None, 1, D_MODEL), lambda i, *_: (layer, 0, 0, part))
    return pl.BlockSpec((None, None, 1, D_MODEL),
                        lambda i, *_: (layer, 1 + (i * tm) // rows_per_mod, 0, part))


def _layer_vec_spec(layer, width):
    return pl.BlockSpec((None, 1, width), lambda *_: (layer, 0, 0))


def _ada_kernel(c_ref, w_ref, b_ref, o_ref):
    c = c_ref[...]
    s = jax.nn.silu(c).astype(BF16)
    o_ref[...] = _dot(s, w_ref[...].astype(BF16)) + b_ref[...]


def _ada_call(c_all, w_ada, b_ada):
    rows = c_all.shape[0]
    tn = 1024
    nj = (6 * D_MODEL) // tn
    return pl.pallas_call(
        _ada_kernel,
        grid=(DEPTH, nj),
        in_specs=[
            pl.BlockSpec((rows, D_MODEL), lambda l, j: (0, 0)),
            pl.BlockSpec((None, D_MODEL, tn), lambda l, j: (l, 0, j)),
            pl.BlockSpec((None, 1, tn), lambda l, j: (l, 0, j)),
        ],
        out_specs=pl.BlockSpec((None, rows, tn), lambda l, j: (l, 0, j)),
        out_shape=jax.ShapeDtypeStruct((DEPTH, rows, 6 * D_MODEL), F32),
        compiler_params=_cparams(("arbitrary", "arbitrary")),
        name="ada_mod",
    )(c_all, w_ada, b_ada.reshape(DEPTH, 1, 6 * D_MODEL))


def _ln_mm_in_kernel(y_ref, lg_ref, lb_ref, sh_ref, sc_ref, w_ref, z_ref, x_ref):
    for s in range(y_ref.shape[0] // LN_SUB):
        rows = pl.ds(s * LN_SUB, LN_SUB)
        x = _layer_norm(y_ref[rows, :], lg_ref[...], lb_ref[...])
        x_ref[rows, :] = x
        h = (x * (1.0 + sc_ref[...]) + sh_ref[...]).astype(BF16)
        z_ref[rows, :] = _dot(h, w_ref[...])


def _ln_mm_in_call(y, ln_g, ln_b, ln_layer, mod4, rows_per_mod, w, layer):
    m = y.shape[0]
    tm = min(512, m)
    row = pl.BlockSpec((tm, D_MODEL), lambda i: (i, 0))
    return pl.pallas_call(
        _ln_mm_in_kernel,
        grid=(m // tm,),
        in_specs=[
            row, _layer_vec_spec(ln_layer, D_MODEL), _layer_vec_spec(ln_layer, D_MODEL),
            _mod_spec(layer, 0, rows_per_mod, tm), _mod_spec(layer, 1, rows_per_mod, tm),
            pl.BlockSpec((None, D_MODEL, Z_WIDTH), lambda i: (layer, 0, 0), pipeline_mode=pl.Buffered(1)),
        ],
        out_specs=[pl.BlockSpec((tm, Z_WIDTH), lambda i: (i, 0)), row],
        out_shape=[jax.ShapeDtypeStruct((m, Z_WIDTH), F32), jax.ShapeDtypeStruct((m, D_MODEL), F32)],
        compiler_params=_cparams(("arbitrary",)),
        name="ln_mm_in",
    )(y, ln_g, ln_b, mod4, mod4, w)


def _gla_kernel(*refs, seq_len, seqs, has_s0, emit_state):
    (q_ref, k_ref, v_ref, g_ref, zg_ref, wg_ref, bg_ref, gn_ref, tl_ref) = refs[:9]
    pos = 9
    if has_s0:
        s0f_ref, s0b_ref = refs[pos:pos + 2]
        pos += 2
    if emit_state:
        pos += 1
    mo_ref = refs[pos]
    pos += 1
    if emit_state:
        st_ref = refs[pos]
        pos += 1
    qef, qeb, o_ref, u_ref, dec_ref = refs[pos:]

    n_blk = seqs * seq_len // P1_BLK
    n_chunks = seqs * seq_len // GLA_CHUNK
    seq_chunks = seq_len // GLA_CHUNK
    c = GLA_CHUNK
    dk = GLA_DK

    brow = lax.broadcasted_iota(jnp.int32, (P1_BLK, P1_BLK), 0)
    bcol = lax.broadcasted_iota(jnp.int32, (P1_BLK, P1_BLK), 1)
    same_chunk = (brow >> CHUNK_SHIFT) == (bcol >> CHUNK_SHIFT)
    mask_f = jnp.logical_and(same_chunk, bcol <= brow)
    mask_b = jnp.logical_and(same_chunk, bcol >= brow)
    row_chunk = lax.broadcasted_iota(jnp.int32, (P1_BLK, dk), 0) >> CHUNK_SHIFT

    def phase_a(i, carry):
        r = pl.multiple_of(i * P1_BLK, P1_BLK)
        rows = pl.ds(r, P1_BLK)
        zg = zg_ref[rows, :].astype(BF16)
        la = _log_sigmoid(_dot(zg, wg_ref[...]) + bg_ref[...]) * (1.0 / GLA_GATE_TEMP)
        pre = _band_sum(tl_ref[...], la)
        n_sub = P1_BLK // c
        tot = jnp.concatenate(
            [jnp.broadcast_to(pre[(s + 1) * c - 1:(s + 1) * c, :], (c, 2 * dk)) for s in range(n_sub)], axis=0)
        b_f = pre[:, :dk]
        tot_f = tot[:, :dk]
        tot_b = tot[:, dk:]
        b_b = tot_b - pre[:, dk:] + la[:, dk:]
        q = q_ref[rows, :] * Q_SCALE
        k = k_ref[rows, :]
        qe_f = (q * jnp.exp(b_f)).astype(BF16)
        qe_b = (q * jnp.exp(b_b)).astype(BF16)
        qef[rows, :] = qe_f
        qeb[rows, :] = qe_b
        ke_f = (k * jnp.exp(-b_f)).astype(BF16)
        ke_b = (k * jnp.exp(-b_b)).astype(BF16)
        a = (jnp.where(mask_f, _dot_nt(qe_f, ke_f), 0.0) + jnp.where(mask_b, _dot_nt(qe_b, ke_b), 0.0))
        v = v_ref[rows, :]
        o_ref[rows, :] = _dot(a.astype(BF16), v.astype(BF16))
        kd_f = k * jnp.exp(tot_f - b_f)
        kd_b = k * jnp.exp(tot_b - b_b)
        kd_cols = [jnp.where(row_chunk == s, kd, 0.0).astype(BF16) for kd in (kd_f, kd_b) for s in range(n_sub)]
        u = _dot(v.T.astype(BF16), jnp.concatenate(kd_cols, axis=-1))
        for s in range(n_sub):
            ch = n_sub * i + s
            u_ref[ch] = u[:, s * dk:(s + 1) * dk]
            u_ref[n_chunks + ch] = u[:, (n_sub + s) * dk:(n_sub + s + 1) * dk]
            dec = jnp.exp(tot[s * c:s * c + 8, :])
            dec_ref[pl.ds(pl.multiple_of(8 * ch, 8), 8), :] = dec[:, :dk]
            dec_ref[pl.ds(pl.multiple_of(8 * (n_chunks + ch), 8), 8), :] = dec[:, dk:]
        return carry

    lax.fori_loop(0, n_blk, phase_a, 0, unroll=max(u for u in (1, 2, 4, 8) if n_blk % u == 0))

    def fwd_chunk(ch, st):
        rows = pl.ds(pl.multiple_of(ch * c, c), c)
        o_ref[rows, :] += _dot_nt(qef[rows, :], st.astype(BF16))
        return st * dec_ref[pl.ds(pl.multiple_of(8 * ch, 8), 1), :] + u_ref[ch]

    def bwd_chunk(j, st, last):
        ch = last - j
        rows = pl.ds(pl.multiple_of(ch * c, c), c)
        o = o_ref[rows, :] + _dot_nt(qeb[rows, :], st.astype(BF16))
        ms = jnp.mean(o * o, axis=-1, keepdims=True)
        y = o * lax.rsqrt(ms + LN_EPS) * gn_ref[...]
        mo_ref[rows, :] = (y * jax.nn.silu(g_ref[rows, :])).astype(BF16)
        return st * dec_ref[pl.ds(pl.multiple_of(8 * (n_chunks + ch), 8), 1), :] + u_ref[n_chunks + ch]

    unroll = True if seq_chunks <= 4 else 16
    for sl in range(seqs):
        first = sl * seq_chunks
        if has_s0:
            st0_f = s0f_ref[...].T
            st0_b = s0b_ref[...].T
        else:
            st0_f = jnp.zeros((GLA_DV, dk), F32)
            st0_b = st0_f
        st_f = lax.fori_loop(first, first + seq_chunks, fwd_chunk, st0_f, unroll=unroll)
        st_b = lax.fori_loop(0, seq_chunks, functools.partial(bwd_chunk, last=first + seq_chunks - 1), st0_b,
                             unroll=unroll)
        if emit_state:
            st_ref[0, sl] = st_f.T
            st_ref[1, sl] = st_b.T


def _chunk_prefix_matrix():
    i = np.arange(P1_BLK)
    same = (i[:, None] >> CHUNK_SHIFT) == (i[None, :] >> CHUNK_SHIFT)
    return jnp.asarray(np.logical_and(same, i[None, :] <= i[:, None]).astype(np.float32), BF16)


def _gla_call(z, wgate, bgate, gnorm, layer, nseq, seq_len, s0=None, state_buf=None):
    m = nseq * seq_len
    tl = _chunk_prefix_matrix()
    has_s0 = s0 is not None
    emit_state = state_buf is not None
    seqs = 1
    if not has_s0:
        seqs = max(d for d in range(1, max(1, GLA_STEP_ROWS // seq_len) + 1) if nseq % d == 0)
    rows = seqs * seq_len
    kw, vw = GLA_DK, GLA_DV
    in_specs = [
        pl.BlockSpec((rows, kw), lambda s, h: (s, Z_Q // kw + h)),
        pl.BlockSpec((rows, kw), lambda s, h: (s, Z_K // kw + h)),
        pl.BlockSpec((rows, vw), lambda s, h: (s, Z_V // vw + h)),
        pl.BlockSpec((rows, vw), lambda s, h: (s, Z_G // vw + h)),
        pl.BlockSpec((rows, Z_GATE_PAD), lambda s, h: (s, Z_GATE // Z_GATE_PAD)),
        pl.BlockSpec((None, Z_GATE_PAD, 2 * kw), lambda s, h: (layer, 0, h)),
        pl.BlockSpec((None, 1, 2 * kw), lambda s, h: (layer, 0, h)),
        pl.BlockSpec((None, 1, vw), lambda s, h: (layer, 0, h)),
        pl.BlockSpec((P1_BLK, P1_BLK), lambda s, h: (0, 0)),
    ]
    args = [z, z, z, z, z, wgate, bgate, gnorm, tl]
    if has_s0:
        in_specs += [
            pl.BlockSpec((None, None, None, None, kw, vw), lambda s, h: (layer, 0, s, h, 0, 0)),
            pl.BlockSpec((None, None, None, None, kw, vw), lambda s, h: (layer, 1, s, h, 0, 0)),
        ]
        args += [s0, s0]
    aliases = {}
    if emit_state:
        aliases = {len(args): 1}
        in_specs.append(pl.BlockSpec(memory_space=pl.ANY))
        args.append(state_buf)
    out_specs = [pl.BlockSpec((rows, vw), lambda s, h: (s, h))]
    out_shape = [jax.ShapeDtypeStruct((m, GLA_WIDTH), BF16)]
    if emit_state:
        out_specs.append(pl.BlockSpec((None, 2, seqs, None, kw, vw), lambda s, h: (layer, 0, s, h, 0, 0)))
        out_shape.append(jax.ShapeDtypeStruct((DEPTH, 2, nseq, GLA_HEADS, kw, vw), F32))
    n_chunks = rows // GLA_CHUNK
    scratch = [
        pltpu.VMEM((rows, kw), BF16),
        pltpu.VMEM((rows, kw), BF16),
        pltpu.VMEM((rows, vw), F32),
        pltpu.VMEM((2 * n_chunks, vw, kw), F32),
        pltpu.VMEM((2 * n_chunks * 8, kw), F32),
    ]
    return pl.pallas_call(
        functools.partial(_gla_kernel, seq_len=seq_len, seqs=seqs, has_s0=has_s0, emit_state=emit_state),
        grid=(nseq // seqs, GLA_HEADS),
        in_specs=in_specs,
        out_specs=out_specs,
        out_shape=out_shape,
        scratch_shapes=scratch,
        input_output_aliases=aliases,
        compiler_params=_cparams(("arbitrary", "arbitrary")),
        name="gla_state" if emit_state else "gla",
    )(*args)


def _window_count(idx, n, w):
    lo = jnp.maximum(idx - w // 2, 0)
    hi = jnp.minimum(idx - w // 2 + w, n)
    return hi - lo


def _pool_project(pooled, gi, w_ref, sc_ref, o_ref, rows):
    cols = slice(gi * POOL_GROUP_DIM, (gi + 1) * POOL_GROUP_DIM)
    y = _dot(pooled.astype(BF16), w_ref[gi]) * sc_ref[:, cols]
    o_ref[rows, cols] = y.astype(BF16)


def _shifted(x, pos, n, d):
    rolled = pltpu.roll(x, (-d) % x.shape[0], 0)
    return jnp.where(jnp.logical_and(pos + d >= 0, pos + d < n), rolled, 0.0)


def _window_sum(x, pos, n, w):
    fwd = bwd = x
    s = 1
    while s < w // 2:
        fwd = fwd + _shifted(fwd, pos, n, s)
        bwd = bwd + _shifted(bwd, pos, n, -s)
        s *= 2
    return fwd + _shifted(bwd, pos, n, -1)


def _pool1d_kernel(p_ref, w_ref, sc_ref, o_ref, *, seq_len):
    t = lax.broadcasted_iota(jnp.int32, (seq_len, 1), 0)
    for sl in range(p_ref.shape[0] // seq_len):
        rows = pl.ds(sl * seq_len, seq_len)
        for gi, w in enumerate(POOL_WINDOWS):
            x = p_ref[rows, gi * POOL_GROUP_DIM:(gi + 1) * POOL_GROUP_DIM]
            cnt = _window_count(t, seq_len, w).astype(F32)
            _pool_project(_window_sum(x, t, seq_len, w) / cnt - x, gi, w_ref, sc_ref, o_ref, rows)


def _pool2d_kernel(p_ref, w_ref, sc_ref, o_ref, cs_ref, *, seq_len):
    n_rows = seq_len // GRID_W
    pad = (max(POOL_WINDOWS) // 2) * GRID_W
    n_blk = seq_len // POOL_BLK
    zeros = jnp.zeros((pad, POOL_GROUP_DIM), F32)
    cs_ref[pl.ds(0, pad), :] = zeros
    cs_ref[pl.ds(pad + seq_len, pad), :] = zeros
    local = lax.broadcasted_iota(jnp.int32, (POOL_BLK, 1), 0)
    col = local & (GRID_W - 1)
    for gi, w in enumerate(POOL_WINDOWS):
        cols = slice(gi * POOL_GROUP_DIM, (gi + 1) * POOL_GROUP_DIM)

        def col_sums(i, carry, w=w, cols=cols):
            r = pl.multiple_of(i * POOL_BLK, POOL_BLK)
            acc = _window_sum(p_ref[pl.ds(r, POOL_BLK), cols], col, GRID_W, w)
            cs_ref[pl.ds(pl.multiple_of(r + pad, GRID_W), POOL_BLK), :] = acc
            return carry

        lax.fori_loop(0, n_blk, col_sums, 0)

        def row_sums(i, carry, w=w, cols=cols, gi=gi):
            r = pl.multiple_of(i * POOL_BLK, POOL_BLK)
            acc = jnp.zeros((POOL_BLK, POOL_GROUP_DIM), F32)
            for j in range(w):
                off = pad + (j - w // 2) * GRID_W
                acc = acc + cs_ref[pl.ds(pl.multiple_of(r + off, GRID_W), POOL_BLK), :]
            grow = (r + local) >> GRID_SHIFT
            cnt = (_window_count(grow, n_rows, w) * _window_count(col, GRID_W, w)).astype(F32)
            x = p_ref[pl.ds(r, POOL_BLK), cols]
            _pool_project(acc / cnt - x, gi, w_ref, sc_ref, o_ref, pl.ds(r, POOL_BLK))
            return carry

        lax.fori_loop(0, n_blk, row_sums, 0, unroll=2)


def _pool_call(z, w_pool, pool_scale, layer, nseq, seq_len, on_grid):
    m = nseq * seq_len
    if on_grid:
        body = functools.partial(_pool2d_kernel, seq_len=seq_len)
        pad = (max(POOL_WINDOWS) // 2) * GRID_W
        scratch = [pltpu.VMEM((seq_len + 2 * pad, POOL_GROUP_DIM), F32)]
    else:
        body = functools.partial(_pool1d_kernel, seq_len=seq_len)
        scratch = []
    seqs = 1 if on_grid else max(d for d in range(1, max(1, POOL_STEP_ROWS // seq_len) + 1) if nseq % d == 0)
    rows = seqs * seq_len
    n_groups = len(POOL_WINDOWS)
    return pl.pallas_call(
        body,
        grid=(nseq // seqs,),
        in_specs=[
            pl.BlockSpec((rows, POOL_WIDTH), lambda s: (s, Z_P // POOL_WIDTH)),
            pl.BlockSpec((None, n_groups, POOL_GROUP_DIM, POOL_GROUP_DIM), lambda s: (layer, 0, 0, 0)),
            _layer_vec_spec(layer, POOL_WIDTH),
        ],
        out_specs=pl.BlockSpec((rows, POOL_WIDTH), lambda s: (s, 0)),
        out_shape=jax.ShapeDtypeStruct((m, POOL_WIDTH), BF16),
        scratch_shapes=scratch,
        compiler_params=_cparams(("arbitrary",)),
        name="pool2d" if on_grid else "pool1d",
    )(z, w_pool, pool_scale)


def _mm_out_ln_kernel(mo_ref, mp_ref, w_ref, x_ref, g1_ref, sh2_ref, sc2_ref, lg_ref, lb_ref,
                      x1_ref, h2_ref):
    for s in range(x_ref.shape[0] // LN_SUB):
        rows = pl.ds(s * LN_SUB, LN_SUB)
        m = _dot(mo_ref[rows, :], w_ref[:GLA_WIDTH, :]) + _dot(mp_ref[rows, :], w_ref[GLA_WIDTH:, :])
        x1 = _layer_norm(ALPHA * x_ref[rows, :] + g1_ref[...] * m, lg_ref[...], lb_ref[...])
        x1_ref[rows, :] = x1
        h2_ref[rows, :] = (x1 * (1.0 + sc2_ref[...]) + sh2_ref[...]).astype(BF16)


def _mm_out_ln_call(mo, mp, w_out, x, mod4, rows_per_mod, ln_g, ln_b, layer):
    m = x.shape[0]
    tm = min(512, m)
    row = pl.BlockSpec((tm, D_MODEL), lambda i: (i, 0))
    half = pl.BlockSpec((tm, GLA_WIDTH), lambda i: (i, 0))
    w_spec = pl.BlockSpec((None, D_MODEL, D_MODEL), lambda i: (layer, 0, 0), pipeline_mode=pl.Buffered(1))
    mod = lambda part: _mod_spec(layer, part, rows_per_mod, tm)
    return pl.pallas_call(
        _mm_out_ln_kernel,
        grid=(m // tm,),
        in_specs=[half, half, w_spec, row, mod(2), mod(3), mod(4),
                  _layer_vec_spec(layer, D_MODEL), _layer_vec_spec(layer, D_MODEL)],
        out_specs=[row, row],
        out_shape=[jax.ShapeDtypeStruct((m, D_MODEL), F32), jax.ShapeDtypeStruct((m, D_MODEL), BF16)],
        compiler_params=_cparams(("arbitrary",)),
        name="mm_out_ln",
    )(mo, mp, w_out, x, mod4, mod4, mod4, ln_g, ln_b)


def _mm_gu_kernel(h_ref, wg_ref, wu_ref, o_ref):
    h = h_ref[...]
    for c in range(o_ref.shape[1] // GU_SUB):
        cols = pl.ds(c * GU_SUB, GU_SUB)
        gate = _dot(h, wg_ref[:, cols].astype(BF16))
        up = _dot(h, wu_ref[:, cols].astype(BF16))
        o_ref[:, cols] = (jax.nn.silu(gate) * up).astype(BF16)


def _mm_gu_call(h, w_gu, layer):
    m = h.shape[0]
    tm = min(1024, m)
    tn = 512
    nj = D_FF // tn
    return pl.pallas_call(
        _mm_gu_kernel,
        grid=(nj, m // tm),
        in_specs=[
            pl.BlockSpec((tm, D_MODEL), lambda j, i: (i, 0)),
            pl.BlockSpec((None, D_MODEL, tn), lambda j, i: (layer, 0, j)),
            pl.BlockSpec((None, D_MODEL, tn), lambda j, i: (layer, 0, nj + j)),
        ],
        out_specs=pl.BlockSpec((tm, tn), lambda j, i: (i, j)),
        out_shape=jax.ShapeDtypeStruct((m, D_FF), BF16),
        compiler_params=_cparams(("arbitrary", "arbitrary")),
        name="mm_gu",
    )(h, w_gu, w_gu)


def _mm_down_kernel(*refs, final_ln):
    a_ref, w_ref, x1_ref, g2_ref = refs[:4]
    if final_ln:
        lg_ref, lb_ref, y_ref = refs[4:]
    else:
        (y_ref,) = refs[4:]
    for s in range(y_ref.shape[0] // LN_SUB):
        rows = pl.ds(s * LN_SUB, LN_SUB)
        y = ALPHA * x1_ref[rows, :] + g2_ref[...] * _dot(a_ref[rows, :], w_ref[...])
        y_ref[rows, :] = _layer_norm(y, lg_ref[...], lb_ref[...]) if final_ln else y


def _mm_down_call(act, w_down, x1, mod4, rows_per_mod, ln_g, ln_b, layer):
    m = x1.shape[0]
    tm = min(512, m)
    final_ln = layer + 1 == DEPTH
    row = pl.BlockSpec((tm, D_MODEL), lambda i: (i, 0))
    in_specs = [
        pl.BlockSpec((tm, D_FF), lambda i: (i, 0)),
        pl.BlockSpec((None, D_FF, D_MODEL), lambda i: (layer, 0, 0), pipeline_mode=pl.Buffered(1)),
        row,
        _mod_spec(layer, 5, rows_per_mod, tm),
    ]
    args = [act, w_down, x1, mod4]
    if final_ln:
        in_specs += [_layer_vec_spec(layer, D_MODEL), _layer_vec_spec(layer, D_MODEL)]
        args += [ln_g, ln_b]
    return pl.pallas_call(
        functools.partial(_mm_down_kernel, final_ln=final_ln),
        grid=(m // tm,),
        in_specs=in_specs,
        out_specs=row,
        out_shape=jax.ShapeDtypeStruct((m, D_MODEL), F32),
        compiler_params=_cparams(("arbitrary",)),
        name="mm_down_ln" if final_ln else "mm_down",
    )(*args)


def _prep_weights(w_in, w_a2, b_a, w_pool, w_out, w_down):
    depth = w_in.shape[0]
    w16 = w_in.astype(BF16)
    gate_pad = jnp.zeros((depth, D_MODEL, Z_GATE_PAD - 2 * GLA_GATE_RANK), BF16)
    w_in_r = jnp.concatenate([w16[:, :, :W_AF], w16[:, :, W_P:], w16[:, :, W_AF:W_P], gate_pad], axis=-1)
    wgate = jnp.zeros((depth, Z_GATE_PAD, GLA_HEADS, 2, GLA_DK), F32)
    per_head = lambda t: t.reshape(depth, GLA_GATE_RANK, GLA_HEADS, GLA_DK)
    wgate = wgate.at[:, :GLA_GATE_RANK, :, 0, :].set(per_head(w_a2[:, 0]))
    wgate = wgate.at[:, GLA_GATE_RANK:2 * GLA_GATE_RANK, :, 1, :].set(per_head(w_a2[:, 1]))
    wgate = wgate.reshape(depth, Z_GATE_PAD, 2 * GLA_KEY_WIDTH)
    bgate = b_a.reshape(depth, 2, GLA_HEADS, GLA_DK).transpose(0, 2, 1, 3).reshape(depth, 1, 2 * GLA_KEY_WIDTH)
    return (w_in_r, wgate.astype(BF16), bgate, w_pool.astype(BF16), w_out.astype(BF16),
            w_down.astype(BF16))


def kernel(x_prompt, x_sample, state_gla, c, c_ctx, ln_in_g, ln_in_b, w_ada, b_ada, w_in, w_a2, b_a,
           gla_norm_g, w_pool, pool_scale, w_out, ln1_g, ln1_b, w_gu, w_down, ln2_g, ln2_b):
    nb_p, len_p, _ = x_prompt.shape
    nb_s, len_s, _ = x_sample.shape
    w_in_r, wgate, bgate, w_pool_b, w_out_b, w_down_b = _prep_weights(
        w_in, w_a2, b_a, w_pool, w_out, w_down)
    gnorm = gla_norm_g.reshape(DEPTH, 1, GLA_WIDTH)
    pscale = pool_scale.reshape(DEPTH, 1, POOL_WIDTH)
    ln1 = (ln1_g.reshape(DEPTH, 1, D_MODEL), ln1_b.reshape(DEPTH, 1, D_MODEL))
    ln2 = (ln2_g.reshape(DEPTH, 1, D_MODEL), ln2_b.reshape(DEPTH, 1, D_MODEL))

    n_mod = -(-(1 + nb_s) // 8) * 8
    c_all = jnp.concatenate(
        [c_ctx[None, :], c, jnp.zeros((n_mod - 1 - nb_s, D_MODEL), c.dtype)], axis=0)
    mod4 = _ada_call(c_all, w_ada, b_ada).reshape(DEPTH, n_mod, 1, 6 * D_MODEL)

    streams = [
        (x_prompt.reshape(nb_p * len_p, D_MODEL), None, nb_p, len_p, False),
        (x_sample.reshape(nb_s * len_s, D_MODEL), len_s, nb_s, len_s, True),
    ]
    results = []
    state_buf = jnp.zeros((DEPTH, 2, nb_p, GLA_HEADS, GLA_DK, GLA_DV), F32)
    ln_in = (ln_in_g.reshape(1, 1, D_MODEL), ln_in_b.reshape(1, 1, D_MODEL))
    for y, rows_per_mod, nseq, seq_len, on_grid in streams:
        for l in range(DEPTH):
            ln_g, ln_b, ln_layer = (ln_in[0], ln_in[1], 0) if l == 0 else (ln2[0], ln2[1], l - 1)
            z, x = _ln_mm_in_call(y, ln_g, ln_b, ln_layer, mod4, rows_per_mod, w_in_r, l)
            if on_grid:
                (mo,) = _gla_call(z, wgate, bgate, gnorm, l, nseq, seq_len, s0=state_gla)
            else:
                mo, state_buf = _gla_call(z, wgate, bgate, gnorm, l, nseq, seq_len, state_buf=state_buf)
            mp = _pool_call(z, w_pool_b, pscale, l, nseq, seq_len, on_grid)
            x1, h2 = _mm_out_ln_call(mo, mp, w_out_b, x, mod4, rows_per_mod, ln1[0], ln1[1], l)
            act = _mm_gu_call(h2, w_gu, l)
            y = _mm_down_call(act, w_down_b, x1, mod4, rows_per_mod, ln2[0], ln2[1], l)
        results.append(y)
    y_prompt = results[0].reshape(nb_p, len_p, D_MODEL)
    y_sample = results[1].reshape(nb_s, len_s, D_MODEL)
    return (y_prompt, y_sample, state_buf.astype(x_prompt.dtype))
```
